```python
import jax, jax.numpy as jnp
from jax import lax
import numpy as np

D_MODEL = 1024
BATCH = 2
SEQ = 16384
DEPTH = 2

A_WIDTH = 512
A_GROUPS = 8
A_CHUNK = 128
B_HEADS = 8
B_NOPE = 64
B_ROPE = 32
B_VDIM = 64
B_QK = B_NOPE + B_ROPE
B_Q_LORA = 384
B_KV_LORA = 256
Q_BLOCK = 128
ROPE_BASE = 10000.0
C_HEADS = 8
C_DK = 64
C_DV = 64
C_CONV = 4
C_CHUNK = 64
C_QKV = C_HEADS * (2 * C_DK + C_DV)
C_Z = C_HEADS * C_DV
N_BRANCH = 3
BRANCH_WIDTH = 512
D_FF = 4 * D_MODEL
EPS = 1e-6
SPLIT_SIZES = (A_WIDTH, A_WIDTH, B_Q_LORA, B_KV_LORA, B_ROPE, C_QKV, C_Z, C_HEADS, C_HEADS, N_BRANCH * D_MODEL)
D_IN = 2 * A_WIDTH + B_Q_LORA + B_KV_LORA + B_ROPE + C_QKV + C_Z + 2 * C_HEADS + N_BRANCH * D_MODEL

kernel_name = "hybrid_gated_gmlp_mla_deltanet"


def rms_norm(x, g):
    xf = x.astype(jnp.float32)
    y = xf * lax.rsqrt(jnp.mean(xf * xf, axis=-1, keepdims=True) + EPS)
    return (y * g.astype(jnp.float32)).astype(x.dtype)


def l2_normalize(x):
    xf = x.astype(jnp.float32)
    return xf * lax.rsqrt(jnp.sum(xf * xf, axis=-1, keepdims=True) + EPS)


def rope_tables(positions):
    half = B_ROPE // 2
    inv_freq = 1.0 / (ROPE_BASE ** (jnp.arange(half, dtype=jnp.float32) / half))
    ang = positions.astype(jnp.float32)[..., None] * inv_freq
    return jnp.cos(ang)[:, :, None, :], jnp.sin(ang)[:, :, None, :]


def rope_tail(x, cos, sin):
    half = B_ROPE // 2
    x_pass = x[..., :B_NOPE]
    x1 = x[..., B_NOPE:B_NOPE + half].astype(jnp.float32)
    x2 = x[..., B_NOPE + half:].astype(jnp.float32)
    rot = jnp.concatenate([x1 * cos - x2 * sin, x1 * sin + x2 * cos], axis=-1)
    return jnp.concatenate([x_pass, rot.astype(x.dtype)], axis=-1)


def spatial_gating_branch(a_u, a_v, sgu_norm_g, w_s, b_s):
    b, s, _ = a_u.shape
    n = s // A_CHUNK
    u = jax.nn.gelu(a_u, approximate=False)
    v = rms_norm(jax.nn.gelu(a_v, approximate=False), sgu_norm_g)
    vg = v.reshape(b, n, A_CHUNK, A_GROUPS, A_WIDTH // A_GROUPS)
    tril = jnp.tril(jnp.ones((A_CHUNK, A_CHUNK), dtype=bool))
    w = jnp.where(tril[None], w_s, jnp.zeros_like(w_s))
    sv = jnp.einsum('gts,bnsgc->bntgc', w, vg) + b_s.T[:, :, None]
    return u * sv.reshape(b, s, A_WIDTH)


def causal_block_attention(q, k, v):
    b, s, h, dq = q.shape
    nb = s // Q_BLOCK
    qb = q.reshape(b, nb, Q_BLOCK, h, dq).transpose(1, 0, 2, 3, 4)
    key_pos = jnp.arange(s)
    scale = dq ** -0.5

    def one_block(args):
        q_blk, i = args
        q_pos = i * Q_BLOCK + jnp.arange(Q_BLOCK)
        sc = jnp.einsum('bqhd,bkhd->bhqk', q_blk, k, preferred_element_type=jnp.float32) * scale
        sc = jnp.where(key_pos[None, :] <= q_pos[:, None], sc, -jnp.inf)
        p = jax.nn.softmax(sc, axis=-1)
        return jnp.einsum('bhqk,bkhv->bqhv', p.astype(v.dtype), v)

    out = lax.map(one_block, (qb, jnp.arange(nb)))
    return out.transpose(1, 0, 2, 3, 4).reshape(b, s, h, v.shape[-1])


def mla_branch(q_lat, kv_lat, k_rope, cos, sin, q_lat_norm_g, w_q_up, kv_lat_norm_g, w_kv_up, q_norm_g, k_norm_g):
    b, s, _ = q_lat.shape
    q = (rms_norm(q_lat, q_lat_norm_g) @ w_q_up).reshape(b, s, B_HEADS, B_QK)
    kv = (rms_norm(kv_lat, kv_lat_norm_g) @ w_kv_up).reshape(b, s, B_HEADS, B_NOPE + B_VDIM)
    k_nope, v = kv[..., :B_NOPE], kv[..., B_NOPE:]
    k = jnp.concatenate([k_nope, jnp.broadcast_to(k_rope[:, :, None, :], (b, s, B_HEADS, B_ROPE))], axis=-1)
    q = rope_tail(rms_norm(q, q_norm_g), cos, sin)
    k = rope_tail(rms_norm(k, k_norm_g), cos, sin)
    o = causal_block_attention(q, k, v)
    return o.reshape(b, s, B_HEADS * B_VDIM)


def causal_depthwise_conv(x, w):
    c = x.shape[-1]
    return lax.conv_general_dilated(x, w[:, None, :].astype(x.dtype), window_strides=(1,),
                                    padding=[(C_CONV - 1, 0)], dimension_numbers=('NWC', 'WIO', 'NWC'),
                                    feature_group_count=c)


def chunked_gated_delta_rule(q, k, v, g, beta):
    b, s, h, dk = q.shape
    dv = v.shape[-1]
    n = s // C_CHUNK
    q = q * (dk ** -0.5)

    def chunk(x):
        return x.reshape(b, n, C_CHUNK, h, -1).transpose(0, 3, 1, 2, 4)

    qc, kc, vc = chunk(q), chunk(k), chunk(v)
    gc = jnp.cumsum(g.reshape(b, n, C_CHUNK, h).transpose(0, 3, 1, 2), axis=-1)
    bc = beta.reshape(b, n, C_CHUNK, h).transpose(0, 3, 1, 2)
    tril = jnp.tril(jnp.ones((C_CHUNK, C_CHUNK), dtype=bool))
    strict = jnp.tril(jnp.ones((C_CHUNK, C_CHUNK), dtype=bool), -1)
    diff = gc[..., :, None] - gc[..., None, :]
    decay = jnp.exp(jnp.where(tril, diff, -jnp.inf))
    k_beta = kc * bc[..., None]
    v_beta = vc * bc[..., None]
    lower = jnp.where(strict, jnp.einsum('bhncd,bhnsd->bhncs', k_beta, kc) * decay, 0.0)
    a_mat = lower + jnp.eye(C_CHUNK, dtype=lower.dtype)
    rhs = jnp.concatenate([v_beta, k_beta * jnp.exp(gc)[..., None]], axis=-1)
    sol = lax.linalg.triangular_solve(a_mat, rhs, left_side=True, lower=True, unit_diagonal=True)
    u, w = sol[..., :dv], sol[..., dv:]
    intra = jnp.where(tril, jnp.einsum('bhncd,bhnsd->bhncs', qc, kc) * decay, 0.0)

    xs = tuple(jnp.moveaxis(t, 2, 0) for t in (qc, kc, u, w, intra, gc))

    def step(state, inp):
        q_i, k_i, u_i, w_i, a_i, g_i = inp
        v_new = u_i - jnp.einsum('bhck,bhkv->bhcv', w_i, state)
        o = (jnp.einsum('bhck,bhkv->bhcv', q_i * jnp.exp(g_i)[..., None], state)
             + jnp.einsum('bhcs,bhsv->bhcv', a_i, v_new))
        g_last = g_i[..., -1]
        state = (state * jnp.exp(g_last)[..., None, None]
                 + jnp.einsum('bhck,bhcv->bhkv', k_i * jnp.exp(g_last[..., None] - g_i)[..., None], v_new))
        return state, o

    state0 = jnp.zeros((b, h, dk, dv), dtype=jnp.float32)
    _, o = lax.scan(step, state0, xs)
    return o.transpose(1, 0, 3, 2, 4).reshape(b, s, h, dv)


def gated_deltanet_branch(c_qkv, c_z, c_b, c_a, conv_w, a_log, dt_bias, o_norm_g):
    b, s, _ = c_qkv.shape
    qkv = jax.nn.silu(causal_depthwise_conv(c_qkv, conv_w))
    q, k, v = jnp.split(qkv, [C_HEADS * C_DK, 2 * C_HEADS * C_DK], axis=-1)
    q = l2_normalize(q.reshape(b, s, C_HEADS, C_DK))
    k = l2_normalize(k.reshape(b, s, C_HEADS, C_DK))
    v = v.reshape(b, s, C_HEADS, C_DV).astype(jnp.float32)
    beta = jax.nn.sigmoid(c_b.astype(jnp.float32))
    g = -jnp.exp(a_log.astype(jnp.float32)) * jax.nn.softplus(c_a.astype(jnp.float32) + dt_bias.astype(jnp.float32))
    o = chunked_gated_delta_rule(q, k, v, g, beta)
    o = rms_norm(o, o_norm_g) * jax.nn.silu(c_z.reshape(b, s, C_HEADS, C_DV).astype(jnp.float32))
    return o.reshape(b, s, C_HEADS * C_DV).astype(c_qkv.dtype)


def setup_inputs(seed: int = 0) -> dict:
    key = jax.random.key(seed)
    ks = jax.random.split(key, 24)
    f32 = jnp.float32

    def nrm(k, shape, scale):
        return jax.random.normal(k, shape, f32) * scale

    def gain(k, shape):
        return 1.0 + 0.02 * jax.random.normal(k, shape, f32)

    x = jax.random.normal(ks[0], (BATCH, SEQ, D_MODEL), f32)
    offset = jax.random.randint(ks[1], (BATCH, 1), 0, 4096, dtype=jnp.int32)
    positions = offset + jnp.arange(SEQ, dtype=jnp.int32)[None, :]
    dt = jnp.exp(jax.random.uniform(ks[15], (DEPTH, C_HEADS), f32) * (jnp.log(0.1) - jnp.log(0.001)) + jnp.log(0.001))
    return {
        "x": x,
        "positions": positions,
        "norm1_g": gain(ks[2], (DEPTH, D_MODEL)),
        "w_in": nrm(ks[3], (DEPTH, D_MODEL, D_IN), D_MODEL ** -0.5),
        "sgu_norm_g": gain(ks[4], (DEPTH, A_WIDTH)),
        "w_spatial": nrm(ks[5], (DEPTH, A_GROUPS, A_CHUNK, A_CHUNK), A_CHUNK ** -0.5),
        "b_spatial": gain(ks[6], (DEPTH, A_GROUPS, A_CHUNK)),
        "q_lat_norm_g": gain(ks[7], (DEPTH, B_Q_LORA)),
        "w_q_up": nrm(ks[8], (DEPTH, B_Q_LORA, B_HEADS * B_QK), B_Q_LORA ** -0.5),
        "kv_lat_norm_g": gain(ks[9], (DEPTH, B_KV_LORA)),
        "w_kv_up": nrm(ks[10], (DEPTH, B_KV_LORA, B_HEADS * (B_NOPE + B_VDIM)), B_KV_LORA ** -0.5),
        "q_norm_g": gain(ks[11], (DEPTH, B_QK)),
        "k_norm_g": gain(ks[12], (DEPTH, B_QK)),
        "conv_w": nrm(ks[13], (DEPTH, C_CONV, C_QKV), C_CONV ** -0.5),
        "a_log": jnp.log(jax.random.uniform(ks[14], (DEPTH, C_HEADS), f32, 1.0, 16.0)),
        "dt_bias": dt + jnp.log(-jnp.expm1(-dt)),
        "o_norm_g": gain(ks[16], (DEPTH, C_DV)),
        "w_branch": nrm(ks[17], (DEPTH, N_BRANCH, BRANCH_WIDTH, D_MODEL), BRANCH_WIDTH ** -0.5),
        "w_out": nrm(ks[18], (DEPTH, D_MODEL, D_MODEL), D_MODEL ** -0.5),
        "norm2_g": gain(ks[19], (DEPTH, D_MODEL)),
        "w_ff1": nrm(ks[20], (DEPTH, D_MODEL, D_FF), D_MODEL ** -0.5),
        "w_ff2": nrm(ks[21], (DEPTH, D_FF, D_MODEL), D_FF ** -0.5),
    }


def reference(x, positions, norm1_g, w_in, sgu_norm_g, w_spatial, b_spatial, q_lat_norm_g, w_q_up,
              kv_lat_norm_g, w_kv_up, q_norm_g, k_norm_g, conv_w, a_log, dt_bias, o_norm_g,
              w_branch, w_out, norm2_g, w_ff1, w_ff2):
    b, s, d = x.shape
    cos, sin = rope_tables(positions)
    split_points = [int(c) for c in np.cumsum(SPLIT_SIZES)[:-1]]
    for l in range(DEPTH):
        h = rms_norm(x, norm1_g[l])
        proj = h @ w_in[l]
        a_u, a_v, q_lat, kv_lat, k_rope, c_qkv, c_z, c_b, c_a, gates = jnp.split(proj, split_points, axis=-1)
        y_a = spatial_gating_branch(a_u, a_v, sgu_norm_g[l], w_spatial[l], b_spatial[l])
        y_b = mla_branch(q_lat, kv_lat, k_rope, cos, sin, q_lat_norm_g[l], w_q_up[l],
                         kv_lat_norm_g[l], w_kv_up[l], q_norm_g[l], k_norm_g[l])
        y_c = gated_deltanet_branch(c_qkv, c_z, c_b, c_a, conv_w[l], a_log[l], dt_bias[l], o_norm_g[l])
        branches = jnp.stack([y_a, y_b, y_c], axis=2)
        y = jnp.einsum('bsnc,ncd->bsnd', branches, w_branch[l])
        gate = jax.nn.sigmoid(gates.reshape(b, s, N_BRANCH, d))
        merged = jnp.einsum('bsnd,bsnd->bsd', gate, y)
        x = x + merged @ w_out[l]
        h2 = rms_norm(x, norm2_g[l])
        x = x + jnp.square(jax.nn.relu(h2 @ w_ff1[l])) @ w_ff2[l]
    return x
```

```python
import functools
import math

import numpy as np
import jax
import jax.numpy as jnp
from jax import lax
from jax.experimental import pallas as pl
from jax.experimental.pallas import tpu as pltpu

F32 = jnp.float32
BF16 = jnp.bfloat16
HI = lax.Precision.HIGHEST

LANE = 128
VMEM_LIMIT = 56 * 1024 * 1024

D = 1024
A_W = 512
A_G = 8
A_T = 128
H = 8
NOPE = 64
ROPE = 32
VD = 64
QK = NOPE + ROPE
Q_LORA = 384
KV_LORA = 256
ROPE_BASE = 10000.0
DK = 64
DV = 64
CONV = 4
CC = 64
NB = 3
BW = 512
DFF = 4 * D
EPS = 1e-6

U_MLA, U_CZ, U_QKV, U_AU, U_AV, U_CB, U_CA, U_GATE = 0, 8, 12, 24, 28, 32, 36, 40
DP = 64 * LANE

O_AU, O_AV, O_QLAT, O_KVLAT, O_KROPE = 0, 512, 1024, 1408, 1664
O_CQKV, O_CZ, O_CB, O_CA, O_GATES = 1696, 3232, 3744, 3752, 3760


def _inproj_columns():
    cols = -np.ones((DP,), np.int64)

    def put(unit, src, n, lane0=0):
        cols[unit * LANE + lane0: unit * LANE + lane0 + n] = np.arange(src, src + n)

    put(U_MLA, O_QLAT, Q_LORA)
    put(U_MLA + 3, O_KVLAT, KV_LORA)
    put(U_MLA + 5, O_KROPE, ROPE, lane0=NOPE)
    put(U_CZ, O_CZ, 512)
    put(U_QKV, O_CQKV, 1536)
    put(U_AU, O_AU, 512)
    put(U_AV, O_AV, 512)
    cols[U_CB * LANE: U_CB * LANE + 512] = O_CB + np.repeat(np.arange(H), DK)
    cols[U_CA * LANE: U_CA * LANE + 512] = O_CA + np.repeat(np.arange(H), DK)
    put(U_GATE, O_GATES, NB * D)
    return cols


_COLS = _inproj_columns()


def _cparams(sem):
    return pltpu.CompilerParams(dimension_semantics=sem, vmem_limit_bytes=VMEM_LIMIT)


def _inproj_kernel(x_ref, g_ref, w_ref, o_ref, h_ref):
    @pl.when(pl.program_id(1) == 0)
    def _():
        x = x_ref[...]
        ms = jnp.mean(x * x, axis=-1, keepdims=True)
        h_ref[...] = (x * lax.rsqrt(ms + EPS) * g_ref[...]).astype(BF16)

    o_ref[...] = jnp.dot(h_ref[...], w_ref[...], preferred_element_type=F32).astype(o_ref.dtype)


def _inproj(x, g, w, tm=1024, tn=1024):
    n = x.shape[0]
    dp = w.shape[1]
    return pl.pallas_call(
        _inproj_kernel,
        grid=(n // tm, dp // tn),
        in_specs=[pl.BlockSpec((tm, D), lambda i, j: (i, 0)),
                  pl.BlockSpec((1, D), lambda i, j: (0, 0)),
                  pl.BlockSpec((D, tn), lambda i, j: (0, j))],
        out_specs=pl.BlockSpec((tm, tn), lambda i, j: (i, j)),
        out_shape=jax.ShapeDtypeStruct((n, dp), BF16),
        scratch_shapes=[pltpu.VMEM((tm, D), BF16)],
        compiler_params=_cparams(("parallel", "arbitrary")),
        name="inproj",
    )(x, g, w)


def _rope_kernel(pos_ref, invf_ref, c_ref, s_ref):
    pos = pos_ref[...].astype(F32)
    ang = pos * invf_ref[...]
    lane = lax.broadcasted_iota(jnp.int32, ang.shape, 1)
    c = jnp.cos(ang)
    s = jnp.sin(ang)
    half = ROPE // 2
    c_ref[...] = jnp.where((lane >= NOPE) & (lane < QK), c, 1.0)
    s_ref[...] = jnp.where(lane < NOPE, 0.0,
                           jnp.where(lane < NOPE + half, -s, jnp.where(lane < QK, s, 0.0)))


def _rope_tables(pos, tm=1024):
    n = pos.shape[0]
    half = ROPE // 2
    lane = np.arange(LANE)
    invf = np.where((lane >= NOPE) & (lane < QK),
                    1.0 / (ROPE_BASE ** (((lane - NOPE) % half).astype(np.float32) / half)), 0.0)
    invf = jnp.asarray(invf, F32).reshape(1, LANE)
    return pl.pallas_call(
        _rope_kernel,
        grid=(n // tm,),
        in_specs=[pl.BlockSpec((tm, 1), lambda i: (i, 0)),
                  pl.BlockSpec((1, LANE), lambda i: (0, 0))],
        out_specs=[pl.BlockSpec((tm, LANE), lambda i: (i, 0)),
                   pl.BlockSpec((tm, LANE), lambda i: (i, 0))],
        out_shape=[jax.ShapeDtypeStruct((n, LANE), F32)] * 2,
        compiler_params=_cparams(("parallel",)),
        name="rope_tables",
    )(pos, invf)


def _gelu(x):
    return 0.5 * x * (1.0 + lax.erf(x * (1.0 / math.sqrt(2.0))))


def _sgu_kernel(u_ref, v_ref, g_ref, w_ref, b_ref, o_ref):
    tm = u_ref.shape[0]
    u = _gelu(u_ref[...].astype(F32))
    v = _gelu(v_ref[...].astype(F32))
    ms = jnp.mean(v * v, axis=-1, keepdims=True)
    v = v * lax.rsqrt(ms + EPS) * g_ref[...]
    lo = (lax.broadcasted_iota(jnp.int32, v.shape, 1) % LANE) < (A_W // A_G)
    v_lo = jnp.where(lo, v, 0.0).astype(BF16)
    v_hi = jnp.where(lo, 0.0, v).astype(BF16)
    r = lax.broadcasted_iota(jnp.int32, (A_T, 2 * A_T), 0)
    c = lax.broadcasted_iota(jnp.int32, (A_T, 2 * A_T), 1)
    causal = (c % A_T) <= r
    for p in range(A_G // 2):
        cols = slice(p * LANE, (p + 1) * LANE)
        w = jnp.where(causal, w_ref[p], 0.0).astype(BF16)
        for ch in range(tm // A_T):
            rows = slice(ch * A_T, (ch + 1) * A_T)
            rhs = jnp.concatenate([v_lo[rows, cols], v_hi[rows, cols]], axis=0)
            sv = jnp.dot(w, rhs, preferred_element_type=F32) + b_ref[:, cols]
            o_ref[rows, cols] = (u[rows, cols] * sv).astype(o_ref.dtype)


def _sgu(proj, g, wcat, bias, tm=512):
    n = proj.shape[0]
    return pl.pallas_call(
        _sgu_kernel,
        grid=(n // tm,),
        in_specs=[pl.BlockSpec((tm, A_W), lambda i: (i, U_AU // 4)),
                  pl.BlockSpec((tm, A_W), lambda i: (i, U_AV // 4)),
                  pl.BlockSpec((1, A_W), lambda i: (0, 0)),
                  pl.BlockSpec((A_G // 2, A_T, 2 * A_T), lambda i: (0, 0, 0)),
                  pl.BlockSpec((A_T, A_W), lambda i: (0, 0))],
        out_specs=pl.BlockSpec((tm, A_W), lambda i: (i, 0)),
        out_shape=jax.ShapeDtypeStruct((n, A_W), BF16),
        compiler_params=_cparams(("parallel",)),
        name="sgu",
    )(proj, proj, g, wcat, bias)


def _rms(x, g):
    ms = jnp.mean(x * x, axis=-1, keepdims=True)
    return x * lax.rsqrt(ms + EPS) * g


def _mla_prep_kernel(p_ref, c_ref, s_ref, gq_ref, gkv_ref, wq_ref, wk_ref, wv_ref, qg_ref, kg_ref,
                     q_ref, k_ref, v_ref):
    p = p_ref[...].astype(F32)
    ql = p[:, :Q_LORA]
    kvl = p[:, Q_LORA:Q_LORA + KV_LORA]
    kr = p[:, Q_LORA + KV_LORA:]
    qn = _rms(ql, gq_ref[...]).astype(BF16)
    kvn = _rms(kvl, gkv_ref[...]).astype(BF16)
    q = jnp.dot(qn, wq_ref[...], preferred_element_type=F32)
    kn = jnp.dot(kvn, wk_ref[...], preferred_element_type=F32)
    v_ref[...] = jnp.dot(kvn, wv_ref[...], preferred_element_type=F32).astype(v_ref.dtype)
    cs = c_ref[...]
    sn = s_ref[...]
    lane = lax.broadcasted_iota(jnp.int32, cs.shape, 1)
    upper = lane >= NOPE + ROPE // 2
    scale = QK ** -0.5

    def head_norm_rope(x, g):
        ss = jnp.sum(x * x, axis=-1, keepdims=True)
        x = x * lax.rsqrt(ss * (1.0 / QK) + EPS) * g
        partner = jnp.where(upper, pltpu.roll(x, ROPE // 2, 1), pltpu.roll(x, LANE - ROPE // 2, 1))
        return x * cs + partner * sn

    for h in range(H):
        cols = slice(h * LANE, (h + 1) * LANE)
        q_ref[:, cols] = (head_norm_rope(q[:, cols], qg_ref[...]) * scale).astype(q_ref.dtype)
        k_ref[:, cols] = head_norm_rope(kn[:, cols] + kr, kg_ref[...]).astype(k_ref.dtype)


def _mla_prep(proj, ctab, stab, gq, gkv, wq, wk, wv, qg, kg, tm=512):
    n = proj.shape[0]
    wm = 6 * LANE
    full = lambda shp: pl.BlockSpec(shp, lambda i: (0,) * len(shp))
    return pl.pallas_call(
        _mla_prep_kernel,
        grid=(n // tm,),
        in_specs=[pl.BlockSpec((tm, wm), lambda i: (i, 0)),
                  pl.BlockSpec((tm, LANE), lambda i: (i, 0)),
                  pl.BlockSpec((tm, LANE), lambda i: (i, 0)),
                  full((1, Q_LORA)), full((1, KV_LORA)),
                  full((Q_LORA, H * LANE)), full((KV_LORA, H * LANE)), full((KV_LORA, H * VD)),
                  full((1, LANE)), full((1, LANE))],
        out_specs=[pl.BlockSpec((tm, H * LANE), lambda i: (i, 0)),
                   pl.BlockSpec((tm, H * LANE), lambda i: (i, 0)),
                   pl.BlockSpec((tm, H * VD), lambda i: (i, 0))],
        out_shape=[jax.ShapeDtypeStruct((n, H * LANE), BF16),
                   jax.ShapeDtypeStruct((n, H * LANE), BF16),
                   jax.ShapeDtypeStruct((n, H * VD), BF16)],
        compiler_params=_cparams(("parallel",)),
        name="mla_prep",
    )(proj, ctab, stab, gq, gkv, wq, wk, wv, qg, kg)


def _attn_kernel(q_ref, k_ref, v_ref, o_ref, *, t):
    qi = pl.program_id(2)
    row = lax.broadcasted_iota(jnp.int32, (t, t), 0)
    col = lax.broadcasted_iota(jnp.int32, (t, t), 1)
    diag_ok = col <= row
    outs = []
    for h in range(2):
        cols = slice(h * LANE, (h + 1) * LANE)
        q = q_ref[:, cols]

        def step(j, carry, masked):
            m, l, acc = carry
            start = pl.multiple_of(j * t, t)
            kb = k_ref[pl.ds(start, t), cols]
            s = lax.dot_general(q, kb, (((1,), (1,)), ((), ())), preferred_element_type=F32)
            if masked:
                s = jnp.where(diag_ok, s, -jnp.inf)
            m_new = jnp.maximum(m, jnp.max(s, axis=-1, keepdims=True))
            alpha = jnp.exp(m - m_new)
            p = jnp.exp(s - m_new)
            l = alpha * l + jnp.sum(p, axis=-1, keepdims=True)
            vb = v_ref[pl.ds(start, t), :]
            acc = alpha * acc + jnp.dot(p.astype(BF16), vb, preferred_element_type=F32)
            return m_new, l, acc

        init = (jnp.full((t, 1), -jnp.inf, F32), jnp.zeros((t, 1), F32), jnp.zeros((t, LANE), F32))
        carry = lax.fori_loop(0, qi, functools.partial(step, masked=False), init)
        _, l, acc = step(qi, carry, True)
        outs.append(acc / l)
    lane = lax.broadcasted_iota(jnp.int32, (t, LANE), 1)
    o_ref[...] = jnp.where(lane < VD, outs[0], outs[1]).astype(o_ref.dtype)


def _attention(q, k, v, batch, seq, t=512):
    n = q.shape[0]
    nq = seq // t
    k3 = k.reshape(batch, seq, H * LANE)
    v3 = v.reshape(batch, seq, H * VD)
    return pl.pallas_call(
        functools.partial(_attn_kernel, t=t),
        grid=(batch, H // 2, nq),
        in_specs=[pl.BlockSpec((t, 2 * LANE), lambda b, hp, i: (b * nq + i, hp)),
                  pl.BlockSpec((None, seq, 2 * LANE), lambda b, hp, i: (b, 0, hp)),
                  pl.BlockSpec((None, seq, 2 * VD), lambda b, hp, i: (b, 0, hp))],
        out_specs=pl.BlockSpec((t, 2 * VD), lambda b, hp, i: (b * nq + i, hp)),
        out_shape=jax.ShapeDtypeStruct((n, H * VD), BF16),
        compiler_params=_cparams(("parallel", "parallel", "arbitrary")),
        name="attention",
    )(q, k3, v3)


GW = 4 * DK


def _blockdiag(x, mask):
    return jnp.where(mask, jnp.concatenate([x] * (GW // CC), axis=0), jnp.zeros((), x.dtype))


def _segsum(x, e):
    return jnp.dot(x, e, preferred_element_type=F32, precision=HI)


def _deltanet_kernel(qkv_ref, z_ref, b_ref, a_ref, cw_ref, alog_ref, dtb_ref, og_ref, e_ref,
                     o_ref, xpad_ref, state_ref, *, tm):
    t_idx = pl.program_id(1)
    nch = tm // CC
    ng = (H * DK) // GW
    width = 3 * H * DK

    @pl.when(t_idx == 0)
    def _():
        xpad_ref[0:8, :] = jnp.zeros((8, width), F32)
        state_ref[...] = jnp.zeros(state_ref.shape, F32)

    @pl.when(t_idx != 0)
    def _():
        xpad_ref[0:8, :] = xpad_ref[tm:tm + 8, :]

    xpad_ref[8:8 + tm, :] = qkv_ref[...].astype(F32)

    xc = jnp.zeros((tm, width), F32)
    for j in range(CONV):
        xc = xc + cw_ref[j:j + 1, :] * xpad_ref[8 - (CONV - 1) + j: 8 - (CONV - 1) + j + tm, :]
    xc = xc * jax.nn.sigmoid(xc)
    e = e_ref[...]
    hw = H * DK
    q = xc[:, :hw]
    k = xc[:, hw:2 * hw]
    v = xc[:, 2 * hw:]
    q = q * lax.rsqrt(_segsum(q * q, e) + EPS) * (DK ** -0.5)
    k = k * lax.rsqrt(_segsum(k * k, e) + EPS)
    beta = jax.nn.sigmoid(b_ref[...].astype(F32))
    sp_in = a_ref[...].astype(F32) + dtb_ref[...]
    softplus = jnp.maximum(sp_in, 0.0) + jnp.log1p(jnp.exp(-jnp.abs(sp_in)))
    g = -jnp.exp(alog_ref[...]) * softplus

    ri = lax.broadcasted_iota(jnp.int32, (tm, tm), 0)
    ci = lax.broadcasted_iota(jnp.int32, (tm, tm), 1)
    cum = jnp.where((ri // CC == ci // CC) & (ci <= ri), 1.0, 0.0).astype(F32)
    gc = jnp.dot(cum, g, preferred_element_type=F32, precision=HI)
    egc = jnp.exp(gc)
    kb = k * beta
    vb = v * beta
    kbg = kb * egc
    qg = q * egc

    r64 = lax.broadcasted_iota(jnp.int32, (CC, GW), 0)
    l64 = lax.broadcasted_iota(jnp.int32, (CC, GW), 1) % CC
    tril = l64 <= r64
    strict = l64 < r64
    eye_t = jnp.where(l64 == r64, 1.0, 0.0).astype(F32)
    rb = lax.broadcasted_iota(jnp.int32, (GW, GW), 0) // CC
    lb = lax.broadcasted_iota(jnp.int32, (GW, GW), 1) // CC
    bd = rb == lb
    ones = jnp.ones((CC, CC), F32)

    states = [state_ref[gi] for gi in range(ng)]
    for ch in range(nch):
        rows = slice(ch * CC, (ch + 1) * CC)
        for gi in range(ng):
            cols = slice(gi * GW, (gi + 1) * GW)
            k_c = k[rows, cols]
            gc_c = gc[rows, cols]
            bdk = _blockdiag(k_c.astype(BF16), bd)
            lhs = jnp.concatenate([kb[rows, cols], q[rows, cols]], axis=0).astype(BF16)
            a2 = lax.dot_general(lhs, bdk, (((1,), (1,)), ((), ())), preferred_element_type=F32)
            gct = jnp.dot(ones, jnp.where(l64 == r64, gc_c, 0.0), preferred_element_type=F32, precision=HI)
            decay = jnp.exp(jnp.where(tril, gc_c - gct, -jnp.inf))
            low = jnp.where(strict, a2[:CC] * decay, 0.0)
            intra = jnp.where(tril, a2[CC:] * decay, 0.0)
            p_acc = eye_t - low
            m_pow = jnp.dot(low, _blockdiag(low, bd), preferred_element_type=F32, precision=HI)
            n_sq = int(math.log2(CC)) - 1
            for lvl in range(n_sq):
                bdm = _blockdiag(m_pow, bd)
                if lvl < n_sq - 1:
                    r = jnp.dot(jnp.concatenate([p_acc, m_pow], axis=0), bdm,
                                preferred_element_type=F32, precision=HI)
                    p_acc = p_acc + r[:CC]
                    m_pow = r[CC:]
                else:
                    p_acc = p_acc + jnp.dot(p_acc, bdm, preferred_element_type=F32, precision=HI)
            rhs = jnp.concatenate([_blockdiag(vb[rows, cols], bd), _blockdiag(kbg[rows, cols], bd)], axis=1)
            uw = jnp.dot(p_acc, rhs, preferred_element_type=F32, precision=HI)
            u = uw[:, :GW]
            w = uw[:, GW:]
            st = states[gi]
            lhs2 = jnp.concatenate([w, qg[rows, cols]], axis=0).astype(BF16)
            ws = jnp.dot(lhs2, st.astype(BF16), preferred_element_type=F32)
            v_new = u - ws[:CC]
            o = ws[CC:] + jnp.dot(intra.astype(BF16), _blockdiag(v_new.astype(BF16), bd),
                                  preferred_element_type=F32)
            g_last = gc[ch * CC + CC - 1: ch * CC + CC, cols]
            kd = (k_c * jnp.exp(g_last - gc_c)).astype(BF16)
            upd = lax.dot_general(kd, v_new.astype(BF16), (((0,), (0,)), ((), ())),
                                  preferred_element_type=F32)
            states[gi] = st * jnp.exp(g_last) + jnp.where(bd, upd, 0.0)
            o_ref[rows, cols] = o.astype(o_ref.dtype)
    for gi in range(ng):
        state_ref[gi] = states[gi]

    o = o_ref[...].astype(F32)
    o = o * lax.rsqrt(_segsum(o * o, e) * (1.0 / DV) + EPS) * og_ref[...]
    z = z_ref[...].astype(F32)
    o_ref[...] = (o * (z * jax.nn.sigmoid(z))).astype(o_ref.dtype)


def _deltanet(proj, conv_w, alog, dtb, og, batch, seq, tm=256):
    n = proj.shape[0]
    nt = seq // tm
    hw = H * DK
    e = jnp.asarray(np.kron(np.eye(H), np.ones((DK, DK))), F32)
    full = lambda shp: pl.BlockSpec(shp, lambda b, i: (0,) * len(shp))
    blk = lambda unit, w: pl.BlockSpec((tm, w), lambda b, i: (b * nt + i, unit * LANE // w))
    return pl.pallas_call(
        functools.partial(_deltanet_kernel, tm=tm),
        grid=(batch, nt),
        in_specs=[blk(U_QKV, 3 * hw), blk(U_CZ, hw), blk(U_CB, hw), blk(U_CA, hw),
                  full((CONV, 3 * hw)), full((1, hw)), full((1, hw)), full((1, hw)), full((hw, hw))],
        out_specs=pl.BlockSpec((tm, hw), lambda b, i: (b * nt + i, 0)),
        out_shape=jax.ShapeDtypeStruct((n, hw), F32),
        scratch_shapes=[pltpu.VMEM((tm + 8, 3 * hw), F32),
                        pltpu.VMEM((hw // GW, GW, GW), F32)],
        compiler_params=_cparams(("parallel", "arbitrary")),
        name="deltanet",
    )(proj, proj, proj, proj, conv_w, alog, dtb, og, e)


def _merge_kernel(x_ref, ya_ref, yb_ref, yc_ref, g0_ref, g1_ref, g2_ref, wb_ref, wo_ref, o_ref):
    merged = None
    for i, (y_ref, g_ref) in enumerate(((ya_ref, g0_ref), (yb_ref, g1_ref), (yc_ref, g2_ref))):
        y = jnp.dot(y_ref[...].astype(BF16), wb_ref[i], preferred_element_type=F32)
        term = jax.nn.sigmoid(g_ref[...].astype(F32)) * y
        merged = term if merged is None else merged + term
    o_ref[...] = x_ref[...] + jnp.dot(merged.astype(BF16), wo_ref[...], preferred_element_type=F32)


def _merge(x, ya, yb, yc, proj, wb, wo, tm=512):
    n = x.shape[0]
    gate = lambda i: pl.BlockSpec((tm, D), lambda r, i=i: (r, U_GATE // 8 + i))
    return pl.pallas_call(
        _merge_kernel,
        grid=(n // tm,),
        in_specs=[pl.BlockSpec((tm, D), lambda r: (r, 0)),
                  pl.BlockSpec((tm, BW), lambda r: (r, 0)),
                  pl.BlockSpec((tm, BW), lambda r: (r, 0)),
                  pl.BlockSpec((tm, BW), lambda r: (r, 0)),
                  gate(0), gate(1), gate(2),
                  pl.BlockSpec((NB, BW, D), lambda r: (0, 0, 0)),
                  pl.BlockSpec((D, D), lambda r: (0, 0))],
        out_specs=pl.BlockSpec((tm, D), lambda r: (r, 0)),
        out_shape=jax.ShapeDtypeStruct((n, D), F32),
        compiler_params=_cparams(("parallel",)),
        name="merge",
    )(x, ya, yb, yc, proj, proj, proj, wb, wo)


def _ffn_kernel(x_ref, g_ref, w1_ref, w2_ref, o_ref, h_ref, acc_ref):
    j = pl.program_id(1)

    @pl.when(j == 0)
    def _():
        h_ref[...] = _rms(x_ref[...], g_ref[...]).astype(BF16)
        acc_ref[...] = jnp.zeros(acc_ref.shape, F32)

    a = jnp.dot(h_ref[...], w1_ref[...], preferred_element_type=F32)
    a = jnp.square(jnp.maximum(a, 0.0)).astype(BF16)
    acc_ref[...] += jnp.dot(a, w2_ref[...], preferred_element_type=F32)

    @pl.when(j == pl.num_programs(1) - 1)
    def _():
        o_ref[...] = x_ref[...] + acc_ref[...]


def _ffn(x, g, w1, w2, tm=1024, tf=1024):
    n = x.shape[0]
    return pl.pallas_call(
        _ffn_kernel,
        grid=(n // tm, DFF // tf),
        in_specs=[pl.BlockSpec((tm, D), lambda i, j: (i, 0)),
                  pl.BlockSpec((1, D), lambda i, j: (0, 0)),
                  pl.BlockSpec((D, tf), lambda i, j: (0, j)),
                  pl.BlockSpec((tf, D), lambda i, j: (j, 0))],
        out_specs=pl.BlockSpec((tm, D), lambda i, j: (i, 0)),
        out_shape=jax.ShapeDtypeStruct((n, D), F32),
        scratch_shapes=[pltpu.VMEM((tm, D), BF16), pltpu.VMEM((tm, D), F32)],
        compiler_params=_cparams(("parallel", "arbitrary")),
        name="ffn",
    )(x, g, w1, w2)


def _pad_heads(w, per_head, used):
    kdim = w.shape[0]
    w = w.reshape(kdim, H, per_head)[:, :, :used]
    w = jnp.pad(w, ((0, 0), (0, 0), (0, LANE - used)))
    return w.reshape(kdim, H * LANE)


def kernel(x, positions, norm1_g, w_in, sgu_norm_g, w_spatial, b_spatial, q_lat_norm_g, w_q_up,
           kv_lat_norm_g, w_kv_up, q_norm_g, k_norm_g, conv_w, a_log, dt_bias, o_norm_g,
           w_branch, w_out, norm2_g, w_ff1, w_ff2):
    batch, seq, d = x.shape
    depth = w_in.shape[0]
    n = batch * seq
    xt = x.reshape(n, d)
    ctab, stab = _rope_tables(positions.reshape(n, 1))
    cols = jnp.asarray(np.maximum(_COLS, 0), jnp.int32)
    colmask = jnp.asarray(_COLS >= 0)
    row = lambda a: a.reshape(1, -1).astype(F32)
    for l in range(depth):
        w_perm = jnp.where(colmask[None, :], jnp.take(w_in[l], cols, axis=1), 0.0).astype(BF16)
        proj = _inproj(xt, row(norm1_g[l]), w_perm)

        wcat = w_spatial[l].reshape(A_G // 2, 2, A_T, A_T).transpose(0, 2, 1, 3).reshape(A_G // 2, A_T, 2 * A_T)
        bias = jnp.repeat(b_spatial[l].T, A_W // A_G, axis=1).astype(F32)
        y_a = _sgu(proj, row(sgu_norm_g[l]), wcat.astype(F32), bias)

        wq = _pad_heads(w_q_up[l], QK, QK).astype(BF16)
        wkv = w_kv_up[l].reshape(KV_LORA, H, NOPE + VD)
        wk = jnp.pad(wkv[:, :, :NOPE], ((0, 0), (0, 0), (0, LANE - NOPE))).reshape(KV_LORA, H * LANE).astype(BF16)
        wv = wkv[:, :, NOPE:].reshape(KV_LORA, H * VD).astype(BF16)
        pad_g = lambda gvec: jnp.pad(gvec.astype(F32), (0, LANE - QK)).reshape(1, LANE)
        q, k, v = _mla_prep(proj, ctab, stab, row(q_lat_norm_g[l]), row(kv_lat_norm_g[l]),
                            wq, wk, wv, pad_g(q_norm_g[l]), pad_g(k_norm_g[l]))
        y_b = _attention(q, k, v, batch, seq)

        rep = lambda a: jnp.repeat(a.astype(F32), DK).reshape(1, H * DK)
        y_c = _deltanet(proj, conv_w[l].astype(F32), rep(a_log[l]), rep(dt_bias[l]),
                        jnp.tile(o_norm_g[l].astype(F32), H).reshape(1, H * DV), batch, seq)

        xt = _merge(xt, y_a, y_b, y_c, proj, w_branch[l].astype(BF16), w_out[l].astype(BF16))
        xt = _ffn(xt, row(norm2_g[l]), w_ff1[l].astype(BF16), w_ff2[l].astype(BF16))
    return xt.reshape(batch, seq, d)
```

```python
import functools
import math

import numpy as np
import jax
import jax.numpy as jnp
from jax import lax
from jax.experimental import pallas as pl
from jax.experimental.pallas import tpu as pltpu

F32 = jnp.float32
BF16 = jnp.bfloat16

LANE = 128
VMEM_LIMIT = 56 * 1024 * 1024

D = 1024
A_W = 512
A_G = 8
A_T = 128
H = 8
NOPE = 64
ROPE = 32
VD = 64
QK = NOPE + ROPE
Q_LORA = 384
KV_LORA = 256
ROPE_BASE = 10000.0
DK = 64
DV = 64
CONV = 4
CC = 64
NB = 3
BW = 512
DFF = 4 * D
EPS = 1e-6

U_MLA, U_CZ, U_QKV, U_AU, U_AV, U_CB, U_CA, U_GATE = 0, 8, 12, 24, 28, 32, 36, 40
DP = 64 * LANE

O_AU, O_AV, O_QLAT, O_KVLAT, O_KROPE = 0, 512, 1024, 1408, 1664
O_CQKV, O_CZ, O_CB, O_CA, O_GATES = 1696, 3232, 3744, 3752, 3760


def _inproj_columns():
    cols = -np.ones((DP,), np.int64)

    def put(unit, src, n, lane0=0):
        cols[unit * LANE + lane0: unit * LANE + lane0 + n] = np.arange(src, src + n)

    put(U_MLA, O_QLAT, Q_LORA)
    put(U_MLA + 3, O_KVLAT, KV_LORA)
    put(U_MLA + 5, O_KROPE, ROPE, lane0=NOPE)
    put(U_CZ, O_CZ, 512)
    put(U_QKV, O_CQKV, 1536)
    put(U_AU, O_AU, 512)
    put(U_AV, O_AV, 512)
    cols[U_CB * LANE: U_CB * LANE + 512] = O_CB + np.repeat(np.arange(H), DK)
    cols[U_CA * LANE: U_CA * LANE + 512] = O_CA + np.repeat(np.arange(H), DK)
    put(U_GATE, O_GATES, NB * D)
    return cols


_COLS = _inproj_columns()


def _cparams(sem):
    return pltpu.CompilerParams(dimension_semantics=sem, vmem_limit_bytes=VMEM_LIMIT)


def _inproj_kernel(x_ref, g_ref, w_ref, o_ref, h_ref):
    @pl.when(pl.program_id(1) == 0)
    def _():
        x = x_ref[...]
        ms = jnp.mean(x * x, axis=-1, keepdims=True)
        h_ref[...] = (x * lax.rsqrt(ms + EPS) * g_ref[...]).astype(BF16)

    o_ref[...] = jnp.dot(h_ref[...], w_ref[...], preferred_element_type=F32).astype(o_ref.dtype)


def _inproj(x, g, w, tm=1024, tn=1024):
    n = x.shape[0]
    dp = w.shape[1]
    return pl.pallas_call(
        _inproj_kernel,
        grid=(n // tm, dp // tn),
        in_specs=[pl.BlockSpec((tm, D), lambda i, j: (i, 0)),
                  pl.BlockSpec((1, D), lambda i, j: (0, 0)),
                  pl.BlockSpec((D, tn), lambda i, j: (0, j))],
        out_specs=pl.BlockSpec((tm, tn), lambda i, j: (i, j)),
        out_shape=jax.ShapeDtypeStruct((n, dp), BF16),
        scratch_shapes=[pltpu.VMEM((tm, D), BF16)],
        compiler_params=_cparams(("parallel", "arbitrary")),
        name="inproj",
    )(x, g, w)


def _rope_kernel(pos_ref, invf_ref, c_ref, s_ref):
    pos = pos_ref[...].astype(F32)
    ang = pos * invf_ref[...]
    lane = lax.broadcasted_iota(jnp.int32, ang.shape, 1)
    c = jnp.cos(ang)
    s = jnp.sin(ang)
    half = ROPE // 2
    c_ref[...] = jnp.where((lane >= NOPE) & (lane < QK), c, 1.0)
    s_ref[...] = jnp.where(lane < NOPE, 0.0,
                           jnp.where(lane < NOPE + half, -s, jnp.where(lane < QK, s, 0.0)))


def _rope_tables(pos, tm=1024):
    n = pos.shape[0]
    half = ROPE // 2
    lane = np.arange(LANE)
    invf = np.where((lane >= NOPE) & (lane < QK),
                    1.0 / (ROPE_BASE ** (((lane - NOPE) % half).astype(np.float32) / half)), 0.0)
    invf = jnp.asarray(invf, F32).reshape(1, LANE)
    return pl.pallas_call(
        _rope_kernel,
        grid=(n // tm,),
        in_specs=[pl.BlockSpec((tm, 1), lambda i: (i, 0)),
                  pl.BlockSpec((1, LANE), lambda i: (0, 0))],
        out_specs=[pl.BlockSpec((tm, LANE), lambda i: (i, 0)),
                   pl.BlockSpec((tm, LANE), lambda i: (i, 0))],
        out_shape=[jax.ShapeDtypeStruct((n, LANE), F32)] * 2,
        compiler_params=_cparams(("parallel",)),
        name="rope_tables",
    )(pos, invf)


def _gelu(x):
    return 0.5 * x * (1.0 + lax.erf(x * (1.0 / math.sqrt(2.0))))


def _sgu_kernel(u_ref, v_ref, g_ref, w_ref, b_ref, o_ref):
    tm = u_ref.shape[0]
    u = _gelu(u_ref[...].astype(F32))
    v = _gelu(v_ref[...].astype(F32))
    ms = jnp.mean(v * v, axis=-1, keepdims=True)
    v = v * lax.rsqrt(ms + EPS) * g_ref[...]
    lo = (lax.broadcasted_iota(jnp.int32, v.shape, 1) % LANE) < (A_W // A_G)
    v_lo = jnp.where(lo, v, 0.0).astype(BF16)
    v_hi = jnp.where(lo, 0.0, v).astype(BF16)
    r = lax.broadcasted_iota(jnp.int32, (A_T, 2 * A_T), 0)
    c = lax.broadcasted_iota(jnp.int32, (A_T, 2 * A_T), 1)
    causal = (c % A_T) <= r
    for p in range(A_G // 2):
        cols = slice(p * LANE, (p + 1) * LANE)
        w = jnp.where(causal, w_ref[p], 0.0).astype(BF16)
        for ch in range(tm // A_T):
            rows = slice(ch * A_T, (ch + 1) * A_T)
            rhs = jnp.concatenate([v_lo[rows, cols], v_hi[rows, cols]], axis=0)
            sv = jnp.dot(w, rhs, preferred_element_type=F32) + b_ref[:, cols]
            o_ref[rows, cols] = (u[rows, cols] * sv).astype(o_ref.dtype)


def _sgu(proj, g, wcat, bias, tm=512):
    n = proj.shape[0]
    return pl.pallas_call(
        _sgu_kernel,
        grid=(n // tm,),
        in_specs=[pl.BlockSpec((tm, A_W), lambda i: (i, U_AU // 4)),
                  pl.BlockSpec((tm, A_W), lambda i: (i, U_AV // 4)),
                  pl.BlockSpec((1, A_W), lambda i: (0, 0)),
                  pl.BlockSpec((A_G // 2, A_T, 2 * A_T), lambda i: (0, 0, 0)),
                  pl.BlockSpec((A_T, A_W), lambda i: (0, 0))],
        out_specs=pl.BlockSpec((tm, A_W), lambda i: (i, 0)),
        out_shape=jax.ShapeDtypeStruct((n, A_W), BF16),
        compiler_params=_cparams(("parallel",)),
        name="sgu",
    )(proj, proj, g, wcat, bias)


def _rms(x, g):
    ms = jnp.mean(x * x, axis=-1, keepdims=True)
    return x * lax.rsqrt(ms + EPS) * g


def _mla_prep_kernel(p_ref, c_ref, s_ref, gq_ref, gkv_ref, wq_ref, wk_ref, wv_ref, qg_ref, kg_ref,
                     q_ref, k_ref, v_ref):
    p = p_ref[...].astype(F32)
    ql = p[:, :Q_LORA]
    kvl = p[:, Q_LORA:Q_LORA + KV_LORA]
    kr = p[:, Q_LORA + KV_LORA:]
    qn = _rms(ql, gq_ref[...]).astype(BF16)
    kvn = _rms(kvl, gkv_ref[...]).astype(BF16)
    q = jnp.dot(qn, wq_ref[...], preferred_element_type=F32)
    kn = jnp.dot(kvn, wk_ref[...], preferred_element_type=F32)
    v = jnp.dot(kvn, wv_ref[...], preferred_element_type=F32)
    vlane = lax.broadcasted_iota(jnp.int32, v.shape, 1) % LANE
    v_ref[...] = jnp.where(vlane == VD, 1.0, v).astype(v_ref.dtype)
    cs = c_ref[...]
    sn = s_ref[...]
    lane = lax.broadcasted_iota(jnp.int32, cs.shape, 1)
    upper = lane >= NOPE + ROPE // 2
    scale = QK ** -0.5 * math.log2(math.e)

    def head_norm_rope(x, g):
        ss = jnp.sum(x * x, axis=-1, keepdims=True)
        x = x * lax.rsqrt(ss * (1.0 / QK) + EPS) * g
        partner = jnp.where(upper, pltpu.roll(x, ROPE // 2, 1), pltpu.roll(x, LANE - ROPE // 2, 1))
        return x * cs + partner * sn

    for h in range(H):
        cols = slice(h * LANE, (h + 1) * LANE)
        q_ref[:, cols] = (head_norm_rope(q[:, cols], qg_ref[...]) * scale).astype(q_ref.dtype)
        k_ref[:, cols] = head_norm_rope(kn[:, cols] + kr, kg_ref[...]).astype(k_ref.dtype)


def _mla_prep(proj, ctab, stab, gq, gkv, wq, wk, wv, qg, kg, tm=512):
    n = proj.shape[0]
    wm = 6 * LANE
    full = lambda shp: pl.BlockSpec(shp, lambda i: (0,) * len(shp))
    return pl.pallas_call(
        _mla_prep_kernel,
        grid=(n // tm,),
        in_specs=[pl.BlockSpec((tm, wm), lambda i: (i, 0)),
                  pl.BlockSpec((tm, LANE), lambda i: (i, 0)),
                  pl.BlockSpec((tm, LANE), lambda i: (i, 0)),
                  full((1, Q_LORA)), full((1, KV_LORA)),
                  full((Q_LORA, H * LANE)), full((KV_LORA, H * LANE)), full((KV_LORA, H * LANE)),
                  full((1, LANE)), full((1, LANE))],
        out_specs=[pl.BlockSpec((tm, H * LANE), lambda i: (i, 0))] * 3,
        out_shape=[jax.ShapeDtypeStruct((n, H * LANE), BF16)] * 3,
        compiler_params=_cparams(("parallel",)),
        name="mla_prep",
    )(proj, ctab, stab, gq, gkv, wq, wk, wv, qg, kg)


def _attn_kernel(q_ref, k_ref, v_ref, o_ref, *, t):
    qi = pl.program_id(2)
    row = lax.broadcasted_iota(jnp.int32, (t, t), 0)
    col = lax.broadcasted_iota(jnp.int32, (t, t), 1)
    diag_ok = col <= row
    heads = (slice(0, LANE), slice(LANE, 2 * LANE))
    qs = [q_ref[:, c] for c in heads]

    def step(j, carry, masked):
        start = pl.multiple_of(j * t, t)
        ss = [lax.dot_general(q, k_ref[pl.ds(start, t), c], (((1,), (1,)), ((), ())),
                              preferred_element_type=F32) for c, q in zip(heads, qs)]
        if masked:
            ss = [jnp.where(diag_ok, s, -jnp.inf) for s in ss]
        ms = [jnp.maximum(m, jnp.max(s, axis=-1, keepdims=True)) for s, (m, _) in zip(ss, carry)]
        ps = [jnp.exp2(s - m_new).astype(BF16) for s, m_new in zip(ss, ms)]
        pv = [jnp.dot(p, v_ref[pl.ds(start, t), c], preferred_element_type=F32) for p, c in zip(ps, heads)]
        return tuple((m_new, jnp.exp2(m - m_new) * acc + x) for m_new, (m, acc), x in zip(ms, carry, pv))

    init = tuple((jnp.full((t, 1), -jnp.inf, F32), jnp.zeros((t, LANE), F32)) for _ in heads)
    carry = lax.fori_loop(0, qi, functools.partial(step, masked=False), init)
    (_, acc0), (_, acc1) = step(qi, carry, True)
    out0 = acc0 / acc0[:, VD:VD + 1]
    out1 = acc1 / acc1[:, VD:VD + 1]
    lane = lax.broadcasted_iota(jnp.int32, (t, LANE), 1)
    o_ref[...] = jnp.where(lane < VD, out0, pltpu.roll(out1, VD, 1)).astype(o_ref.dtype)


def _attention(q, k, v, batch, seq, t=1024):
    n = q.shape[0]
    nq = seq // t
    k3 = k.reshape(batch, seq, H * LANE)
    v3 = v.reshape(batch, seq, H * LANE)
    return pl.pallas_call(
        functools.partial(_attn_kernel, t=t),
        grid=(batch, H // 2, nq),
        in_specs=[pl.BlockSpec((t, 2 * LANE), lambda b, hp, i: (b * nq + i, hp)),
                  pl.BlockSpec((None, seq, 2 * LANE), lambda b, hp, i: (b, 0, hp)),
                  pl.BlockSpec((None, seq, 2 * LANE), lambda b, hp, i: (b, 0, hp))],
        out_specs=pl.BlockSpec((t, 2 * VD), lambda b, hp, i: (b * nq + i, hp)),
        out_shape=jax.ShapeDtypeStruct((n, H * VD), BF16),
        compiler_params=_cparams(("parallel", "parallel", "arbitrary")),
        name="attention",
    )(q, k3, v3)


GW = 4 * DK


def _blockdiag(x, mask):
    return jnp.where(mask, jnp.concatenate([x] * (GW // CC), axis=0), jnp.zeros((), x.dtype))


def _split2(x):
    hi = x.astype(BF16)
    lo = (x - hi.astype(F32)).astype(BF16)
    return hi, lo


def _split3(x):
    hi = x.astype(BF16)
    r = x - hi.astype(F32)
    mid = r.astype(BF16)
    lo = (r - mid.astype(F32)).astype(BF16)
    return hi, mid, lo


def _dot_exact_rhs(sel, x):
    n = x.shape[1]
    r = jnp.dot(sel, jnp.concatenate(_split3(x), axis=1), preferred_element_type=F32)
    return r[:, :n] + r[:, n:2 * n] + r[:, 2 * n:]


def _dot3(a, b_hi, b_lo):
    a_hi, a_lo = _split2(a)
    m = a.shape[0]
    r = jnp.dot(jnp.concatenate([a_hi, a_lo], axis=0), b_hi, preferred_element_type=F32)
    return r[:m] + r[m:] + jnp.dot(a_hi, b_lo, preferred_element_type=F32)


def _segsum(x, e):
    hi, lo = _split2(x)
    m = x.shape[0]
    r = jnp.dot(jnp.concatenate([hi, lo], axis=0), e, preferred_element_type=F32)
    return r[:m] + r[m:]


def _deltanet_kernel(qkv_ref, z_ref, b_ref, a_ref, cw_ref, alog_ref, dtb_ref, og_ref, e_ref,
                     o_ref, xpad_ref, state_ref, *, tm):
    t_idx = pl.program_id(1)
    nch = tm // CC
    ng = (H * DK) // GW
    width = 3 * H * DK

    @pl.when(t_idx == 0)
    def _():
        xpad_ref[0:8, :] = jnp.zeros((8, width), F32)
        state_ref[...] = jnp.zeros(state_ref.shape, F32)

    @pl.when(t_idx != 0)
    def _():
        xpad_ref[0:8, :] = xpad_ref[tm:tm + 8, :]

    xpad_ref[8:8 + tm, :] = qkv_ref[...].astype(F32)

    xc = jnp.zeros((tm, width), F32)
    for j in range(CONV):
        xc = xc + cw_ref[j:j + 1, :] * xpad_ref[8 - (CONV - 1) + j: 8 - (CONV - 1) + j + tm, :]
    xc = xc * jax.nn.sigmoid(xc)
    e = e_ref[...]
    hw = H * DK
    q = xc[:, :hw]
    k = xc[:, hw:2 * hw]
    v = xc[:, 2 * hw:]
    q = q * lax.rsqrt(_segsum(q * q, e) + EPS) * (DK ** -0.5)
    k = k * lax.rsqrt(_segsum(k * k, e) + EPS)
    beta = jax.nn.sigmoid(b_ref[...].astype(F32))
    sp_in = a_ref[...].astype(F32) + dtb_ref[...]
    softplus = jnp.maximum(sp_in, 0.0) + jnp.log1p(jnp.exp(-jnp.abs(sp_in)))
    g = -jnp.exp(alog_ref[...]) * softplus

    ri = lax.broadcasted_iota(jnp.int32, (tm, tm), 0)
    ci = lax.broadcasted_iota(jnp.int32, (tm, tm), 1)
    cum = jnp.where((ri // CC == ci // CC) & (ci <= ri), 1.0, 0.0).astype(BF16)
    gc = _dot_exact_rhs(cum, g)
    egc = jnp.exp(gc)
    kb = k * beta
    vb = v * beta
    kbg = kb * egc
    qg = q * egc

    r64 = lax.broadcasted_iota(jnp.int32, (CC, GW), 0)
    l64 = lax.broadcasted_iota(jnp.int32, (CC, GW), 1) % CC
    tril = l64 <= r64
    strict = l64 < r64
    eye_t = jnp.where(l64 == r64, 1.0, 0.0).astype(F32)
    rb = lax.broadcasted_iota(jnp.int32, (GW, GW), 0) // CC
    lb = lax.broadcasted_iota(jnp.int32, (GW, GW), 1) // CC
    bd = rb == lb
    ones = jnp.ones((CC, CC), BF16)

    tiles = [(slice(ch * CC, (ch + 1) * CC), slice(gi * GW, (gi + 1) * GW), ch, gi)
             for ch in range(nch) for gi in range(ng)]
    low_l, intra_l = [], []
    for rows, cols, _, _ in tiles:
        gc_c = gc[rows, cols]
        bdk = _blockdiag(k[rows, cols].astype(BF16), bd)
        lhs = jnp.concatenate([kb[rows, cols], q[rows, cols]], axis=0).astype(BF16)
        a2 = lax.dot_general(lhs, bdk, (((1,), (1,)), ((), ())), preferred_element_type=F32)
        gct = _dot_exact_rhs(ones, jnp.where(l64 == r64, gc_c, 0.0))
        decay = jnp.exp(jnp.where(tril, gc_c - gct, -jnp.inf))
        low_l.append(jnp.where(strict, a2[:CC] * decay, 0.0))
        intra_l.append(jnp.where(tril, a2[CC:] * decay, 0.0).astype(BF16))
    p_l = [eye_t - low for low in low_l]
    m_l = [_dot3(low, *[_blockdiag(piece, bd) for piece in _split2(low)]) for low in low_l]
    n_sq = int(math.log2(CC)) - 1
    for lvl in range(n_sq):
        for i in range(len(tiles)):
            bdm = [_blockdiag(piece, bd) for piece in _split2(m_l[i])]
            if lvl < n_sq - 1:
                r = _dot3(jnp.concatenate([p_l[i], m_l[i]], axis=0), *bdm)
                p_l[i] = p_l[i] + r[:CC]
                m_l[i] = r[CC:]
            else:
                p_l[i] = p_l[i] + _dot3(p_l[i], *bdm)
    uw_l = []
    for (rows, cols, _, _), p_acc in zip(tiles, p_l):
        rhs = [jnp.concatenate([_blockdiag(a, bd), _blockdiag(b, bd)], axis=1)
               for a, b in zip(_split2(vb[rows, cols]), _split2(kbg[rows, cols]))]
        uw_l.append(_dot3(p_acc, *rhs))

    states = [state_ref[gi] for gi in range(ng)]
    for (rows, cols, ch, gi), uw, intra in zip(tiles, uw_l, intra_l):
        st = states[gi]
        lhs2 = jnp.concatenate([uw[:, GW:], qg[rows, cols]], axis=0).astype(BF16)
        ws = jnp.dot(lhs2, st.astype(BF16), preferred_element_type=F32)
        v_new = (uw[:, :GW] - ws[:CC]).astype(BF16)
        o = ws[CC:] + jnp.dot(intra, _blockdiag(v_new, bd), preferred_element_type=F32)
        g_last = gc[ch * CC + CC - 1: ch * CC + CC, cols]
        kd = (k[rows, cols] * jnp.exp(g_last - gc[rows, cols])).astype(BF16)
        upd = lax.dot_general(kd, v_new, (((0,), (0,)), ((), ())), preferred_element_type=F32)
        states[gi] = st * jnp.exp(g_last) + jnp.where(bd, upd, 0.0)
        o_ref[rows, cols] = o.astype(o_ref.dtype)
    for gi in range(ng):
        state_ref[gi] = states[gi]

    o = o_ref[...].astype(F32)
    o = o * lax.rsqrt(_segsum(o * o, e) * (1.0 / DV) + EPS) * og_ref[...]
    z = z_ref[...].astype(F32)
    o_ref[...] = (o * (z * jax.nn.sigmoid(z))).astype(o_ref.dtype)


def _deltanet(proj, conv_w, alog, dtb, og, batch, seq, tm=256):
    n = proj.shape[0]
    nt = seq // tm
    hw = H * DK
    e = jnp.asarray(np.kron(np.eye(H), np.ones((DK, DK))), BF16)
    full = lambda shp: pl.BlockSpec(shp, lambda b, i: (0,) * len(shp))
    blk = lambda unit, w: pl.BlockSpec((tm, w), lambda b, i: (b * nt + i, unit * LANE // w))
    return pl.pallas_call(
        functools.partial(_deltanet_kernel, tm=tm),
        grid=(batch, nt),
        in_specs=[blk(U_QKV, 3 * hw), blk(U_CZ, hw), blk(U_CB, hw), blk(U_CA, hw),
                  full((CONV, 3 * hw)), full((1, hw)), full((1, hw)), full((1, hw)), full((hw, hw))],
        out_specs=pl.BlockSpec((tm, hw), lambda b, i: (b * nt + i, 0)),
        out_shape=jax.ShapeDtypeStruct((n, hw), F32),
        scratch_shapes=[pltpu.VMEM((tm + 8, 3 * hw), F32),
                        pltpu.VMEM((hw // GW, GW, GW), F32)],
        compiler_params=_cparams(("parallel", "arbitrary")),
        name="deltanet",
    )(proj, proj, proj, proj, conv_w, alog, dtb, og, e)


def _merge_kernel(x_ref, ya_ref, yb_ref, yc_ref, g0_ref, g1_ref, g2_ref, wb_ref, wo_ref, o_ref):
    merged = None
    for i, (y_ref, g_ref) in enumerate(((ya_ref, g0_ref), (yb_ref, g1_ref), (yc_ref, g2_ref))):
        y = jnp.dot(y_ref[...].astype(BF16), wb_ref[i], preferred_element_type=F32)
        term = jax.nn.sigmoid(g_ref[...].astype(F32)) * y
        merged = term if merged is None else merged + term
    o_ref[...] = x_ref[...] + jnp.dot(merged.astype(BF16), wo_ref[...], preferred_element_type=F32)


def _merge(x, ya, yb, yc, proj, wb, wo, tm=512):
    n = x.shape[0]
    gate = lambda i: pl.BlockSpec((tm, D), lambda r, i=i: (r, U_GATE // 8 + i))
    return pl.pallas_call(
        _merge_kernel,
        grid=(n // tm,),
        in_specs=[pl.BlockSpec((tm, D), lambda r: (r, 0)),
                  pl.BlockSpec((tm, BW), lambda r: (r, 0)),
                  pl.BlockSpec((tm, BW), lambda r: (r, 0)),
                  pl.BlockSpec((tm, BW), lambda r: (r, 0)),
                  gate(0), gate(1), gate(2),
                  pl.BlockSpec((NB, BW, D), lambda r: (0, 0, 0)),
                  pl.BlockSpec((D, D), lambda r: (0, 0))],
        out_specs=pl.BlockSpec((tm, D), lambda r: (r, 0)),
        out_shape=jax.ShapeDtypeStruct((n, D), F32),
        compiler_params=_cparams(("parallel",)),
        name="merge",
    )(x, ya, yb, yc, proj, proj, proj, wb, wo)


def _ffn_kernel(x_ref, g_ref, w1_ref, w2_ref, o_ref, h_ref, acc_ref):
    j = pl.program_id(1)

    @pl.when(j == 0)
    def _():
        h_ref[...] = _rms(x_ref[...], g_ref[...]).astype(BF16)
        acc_ref[...] = jnp.zeros(acc_ref.shape, F32)

    a = jnp.dot(h_ref[...], w1_ref[...], preferred_element_type=F32)
    a = jnp.square(jnp.maximum(a, 0.0)).astype(BF16)
    acc_ref[...] += jnp.dot(a, w2_ref[...], preferred_element_type=F32)

    @pl.when(j == pl.num_programs(1) - 1)
    def _():
        o_ref[...] = x_ref[...] + acc_ref[...]


def _ffn(x, g, w1, w2, tm=1024, tf=1024):
    n = x.shape[0]
    return pl.pallas_call(
        _ffn_kernel,
        grid=(n // tm, DFF // tf),
        in_specs=[pl.BlockSpec((tm, D), lambda i, j: (i, 0)),
                  pl.BlockSpec((1, D), lambda i, j: (0, 0)),
                  pl.BlockSpec((D, tf), lambda i, j: (0, j)),
                  pl.BlockSpec((tf, D), lambda i, j: (j, 0))],
        out_specs=pl.BlockSpec((tm, D), lambda i, j: (i, 0)),
        out_shape=jax.ShapeDtypeStruct((n, D), F32),
        scratch_shapes=[pltpu.VMEM((tm, D), BF16), pltpu.VMEM((tm, D), F32)],
        compiler_params=_cparams(("parallel", "arbitrary")),
        name="ffn",
    )(x, g, w1, w2)


def _pad_heads(w, per_head, used):
    kdim = w.shape[0]
    w = w.reshape(kdim, H, per_head)[:, :, :used]
    w = jnp.pad(w, ((0, 0), (0, 0), (0, LANE - used)))
    return w.reshape(kdim, H * LANE)


def kernel(x, positions, norm1_g, w_in, sgu_norm_g, w_spatial, b_spatial, q_lat_norm_g, w_q_up,
           kv_lat_norm_g, w_kv_up, q_norm_g, k_norm_g, conv_w, a_log, dt_bias, o_norm_g,
           w_branch, w_out, norm2_g, w_ff1, w_ff2):
    batch, seq, d = x.shape
    depth = w_in.shape[0]
    n = batch * seq
    xt = x.reshape(n, d)
    ctab, stab = _rope_tables(positions.reshape(n, 1))
    cols = jnp.asarray(np.maximum(_COLS, 0), jnp.int32)
    colmask = jnp.asarray(_COLS >= 0)
    row = lambda a: a.reshape(1, -1).astype(F32)
    for l in range(depth):
        w_perm = jnp.where(colmask[None, :], jnp.take(w_in[l], cols, axis=1), 0.0).astype(BF16)
        proj = _inproj(xt, row(norm1_g[l]), w_perm)

        wcat = w_spatial[l].reshape(A_G // 2, 2, A_T, A_T).transpose(0, 2, 1, 3).reshape(A_G // 2, A_T, 2 * A_T)
        bias = jnp.repeat(b_spatial[l].T, A_W // A_G, axis=1).astype(F32)
        y_a = _sgu(proj, row(sgu_norm_g[l]), wcat.astype(F32), bias)

        wq = _pad_heads(w_q_up[l], QK, QK).astype(BF16)
        wkv = w_kv_up[l].reshape(KV_LORA, H, NOPE + VD)
        wk = jnp.pad(wkv[:, :, :NOPE], ((0, 0), (0, 0), (0, LANE - NOPE))).reshape(KV_LORA, H * LANE).astype(BF16)
        wv = jnp.pad(wkv[:, :, NOPE:], ((0, 0), (0, 0), (0, LANE - VD))).reshape(KV_LORA, H * LANE).astype(BF16)
        pad_g = lambda gvec: jnp.pad(gvec.astype(F32), (0, LANE - QK)).reshape(1, LANE)
        q, k, v = _mla_prep(proj, ctab, stab, row(q_lat_norm_g[l]), row(kv_lat_norm_g[l]),
                            wq, wk, wv, pad_g(q_norm_g[l]), pad_g(k_norm_g[l]))
        y_b = _attention(q, k, v, batch, seq)

        rep = lambda a: jnp.repeat(a.astype(F32), DK).reshape(1, H * DK)
        y_c = _deltanet(proj, conv_w[l].astype(F32), rep(a_log[l]), rep(dt_bias[l]),
                        jnp.tile(o_norm_g[l].astype(F32), H).reshape(1, H * DV), batch, seq)

        xt = _merge(xt, y_a, y_b, y_c, proj, w_branch[l].astype(BF16), w_out[l].astype(BF16))
        xt = _ffn(xt, row(norm2_g[l]), w_ff1[l].astype(BF16), w_ff2[l].astype(BF16))
    return xt.reshape(batch, seq, d)
```

```python
import functools
import math

import numpy as np
import jax
import jax.numpy as jnp
from jax import lax
from jax.experimental import pallas as pl
from jax.experimental.pallas import tpu as pltpu

F32 = jnp.float32
BF16 = jnp.bfloat16

LANE = 128
VMEM_LIMIT = 56 * 1024 * 1024

D = 1024
A_W = 512
A_G = 8
A_T = 128
H = 8
NOPE = 64
ROPE = 32
VD = 64
QK = NOPE + ROPE
Q_LORA = 384
KV_LORA = 256
ROPE_BASE = 10000.0
DK = 64
DV = 64
CONV = 4
CC = 64
NB = 3
BW = 512
DFF = 4 * D
EPS = 1e-6

U_MLA, U_CZ, U_QKV, U_AU, U_AV, U_CB, U_CA, U_GATE = 0, 8, 12, 24, 28, 32, 36, 40
DP = 64 * LANE

O_AU, O_AV, O_QLAT, O_KVLAT, O_KROPE = 0, 512, 1024, 1408, 1664
O_CQKV, O_CZ, O_CB, O_CA, O_GATES = 1696, 3232, 3744, 3752, 3760


def _permute_w_in(w):
    kdim = w.shape[0]
    sl = lambda off, n: w[:, off:off + n]
    zeros = lambda n: jnp.zeros((kdim, n), w.dtype)
    pieces = [
        sl(O_QLAT, Q_LORA), sl(O_KVLAT, KV_LORA),
        zeros(NOPE), sl(O_KROPE, ROPE), zeros(LANE - QK),
        zeros((U_CZ - U_MLA - 6) * LANE),
        sl(O_CZ, 512), sl(O_CQKV, 1536), sl(O_AU, A_W), sl(O_AV, A_W),
        jnp.repeat(sl(O_CB, H), DK, axis=1),
        jnp.repeat(sl(O_CA, H), DK, axis=1),
        sl(O_GATES, NB * D),
    ]
    out = jnp.concatenate(pieces, axis=1).astype(BF16)
    assert out.shape == (kdim, DP)
    return out


def _cparams(sem):
    return pltpu.CompilerParams(dimension_semantics=sem, vmem_limit_bytes=VMEM_LIMIT)


def _inproj_kernel(x_ref, g_ref, w_ref, o_ref, *, tn):
    x = x_ref[...]
    ms = jnp.mean(x * x, axis=-1, keepdims=True)
    h = (x * lax.rsqrt(ms + EPS) * g_ref[...]).astype(BF16)
    for j in range(w_ref.shape[1] // tn):
        cols = slice(j * tn, (j + 1) * tn)
        o_ref[:, cols] = jnp.dot(h, w_ref[:, cols], preferred_element_type=F32).astype(o_ref.dtype)


def _inproj(x, g, w, tm=512, tn=1024):
    n = x.shape[0]
    dp = w.shape[1]
    return pl.pallas_call(
        functools.partial(_inproj_kernel, tn=tn),
        grid=(n // tm,),
        in_specs=[pl.BlockSpec((tm, D), lambda i: (i, 0)),
                  pl.BlockSpec((1, D), lambda i: (0, 0)),
                  pl.BlockSpec((D, dp), lambda i: (0, 0), pipeline_mode=pl.Buffered(1))],
        out_specs=pl.BlockSpec((tm, dp), lambda i: (i, 0)),
        out_shape=jax.ShapeDtypeStruct((n, dp), BF16),
        compiler_params=_cparams(("parallel",)),
        name="inproj",
    )(x, g, w)


def _rope_kernel(pos_ref, invf_ref, c_ref, s_ref):
    pos = pos_ref[...].astype(F32)
    ang = pos * invf_ref[...]
    lane = lax.broadcasted_iota(jnp.int32, ang.shape, 1)
    c = jnp.cos(ang)
    s = jnp.sin(ang)
    half = ROPE // 2
    c_ref[...] = jnp.where((lane >= NOPE) & (lane < QK), c, 1.0)
    s_ref[...] = jnp.where(lane < NOPE, 0.0,
                           jnp.where(lane < NOPE + half, -s, jnp.where(lane < QK, s, 0.0)))


def _rope_tables(pos, tm=1024):
    n = pos.shape[0]
    half = ROPE // 2
    lane = np.arange(LANE)
    invf = np.where((lane >= NOPE) & (lane < QK),
                    1.0 / (ROPE_BASE ** (((lane - NOPE) % half).astype(np.float32) / half)), 0.0)
    invf = jnp.asarray(invf, F32).reshape(1, LANE)
    return pl.pallas_call(
        _rope_kernel,
        grid=(n // tm,),
        in_specs=[pl.BlockSpec((tm, 1), lambda i: (i, 0)),
                  pl.BlockSpec((1, LANE), lambda i: (0, 0))],
        out_specs=[pl.BlockSpec((tm, LANE), lambda i: (i, 0)),
                   pl.BlockSpec((tm, LANE), lambda i: (i, 0))],
        out_shape=[jax.ShapeDtypeStruct((n, LANE), F32)] * 2,
        compiler_params=_cparams(("parallel",)),
        name="rope_tables",
    )(pos, invf)


def _gelu(x):
    return 0.5 * x * (1.0 + lax.erf(x * (1.0 / math.sqrt(2.0))))


def _sgu_kernel(u_ref, v_ref, g_ref, w_ref, b_ref, o_ref):
    tm = u_ref.shape[0]
    u = _gelu(u_ref[...].astype(F32))
    v = _gelu(v_ref[...].astype(F32))
    ms = jnp.mean(v * v, axis=-1, keepdims=True)
    v = v * lax.rsqrt(ms + EPS) * g_ref[...]
    lo = (lax.broadcasted_iota(jnp.int32, v.shape, 1) % LANE) < (A_W // A_G)
    v_lo = jnp.where(lo, v, 0.0).astype(BF16)
    v_hi = jnp.where(lo, 0.0, v).astype(BF16)
    r = lax.broadcasted_iota(jnp.int32, (A_T, 2 * A_T), 0)
    c = lax.broadcasted_iota(jnp.int32, (A_T, 2 * A_T), 1)
    causal = (c % A_T) <= r
    for p in range(A_G // 2):
        cols = slice(p * LANE, (p + 1) * LANE)
        w = jnp.where(causal, w_ref[p], 0.0).astype(BF16)
        for ch in range(tm // A_T):
            rows = slice(ch * A_T, (ch + 1) * A_T)
            rhs = jnp.concatenate([v_lo[rows, cols], v_hi[rows, cols]], axis=0)
            sv = jnp.dot(w, rhs, preferred_element_type=F32) + b_ref[:, cols]
            o_ref[rows, cols] = (u[rows, cols] * sv).astype(o_ref.dtype)


def _sgu(proj, g, wcat, bias, tm=512):
    n = proj.shape[0]
    return pl.pallas_call(
        _sgu_kernel,
        grid=(n // tm,),
        in_specs=[pl.BlockSpec((tm, A_W), lambda i: (i, U_AU // 4)),
                  pl.BlockSpec((tm, A_W), lambda i: (i, U_AV // 4)),
                  pl.BlockSpec((1, A_W), lambda i: (0, 0)),
                  pl.BlockSpec((A_G // 2, A_T, 2 * A_T), lambda i: (0, 0, 0)),
                  pl.BlockSpec((A_T, A_W), lambda i: (0, 0))],
        out_specs=pl.BlockSpec((tm, A_W), lambda i: (i, 0)),
        out_shape=jax.ShapeDtypeStruct((n, A_W), BF16),
        compiler_params=_cparams(("parallel",)),
        name="sgu",
    )(proj, proj, g, wcat, bias)


def _rms(x, g):
    ms = jnp.mean(x * x, axis=-1, keepdims=True)
    return x * lax.rsqrt(ms + EPS) * g


def _mla_prep_kernel(p_ref, c_ref, s_ref, gq_ref, gkv_ref, wq_ref, wk_ref, wv_ref, qg_ref, kg_ref,
                     q_ref, k_ref, v_ref):
    p = p_ref[...].astype(F32)
    ql = p[:, :Q_LORA]
    kvl = p[:, Q_LORA:Q_LORA + KV_LORA]
    kr = p[:, Q_LORA + KV_LORA:]
    qn = _rms(ql, gq_ref[...]).astype(BF16)
    kvn = _rms(kvl, gkv_ref[...]).astype(BF16)
    q = jnp.dot(qn, wq_ref[...], preferred_element_type=F32)
    kn = jnp.dot(kvn, wk_ref[...], preferred_element_type=F32)
    v = jnp.dot(kvn, wv_ref[...], preferred_element_type=F32)
    vlane = lax.broadcasted_iota(jnp.int32, v.shape, 1) % LANE
    v_ref[...] = jnp.where(vlane == VD, 1.0, v).astype(v_ref.dtype)
    cs = c_ref[...]
    sn = s_ref[...]
    lane = lax.broadcasted_iota(jnp.int32, cs.shape, 1)
    upper = lane >= NOPE + ROPE // 2
    scale = QK ** -0.5 * math.log2(math.e)

    q_tabs = (cs * qg_ref[0:1, :], sn * qg_ref[1:2, :])
    k_tabs = (cs * kg_ref[0:1, :], sn * kg_ref[1:2, :])

    tiles = []
    for h in range(H):
        cols = slice(h * LANE, (h + 1) * LANE)
        tiles.append((q_ref, cols, q[:, cols], q_tabs, scale))
        tiles.append((k_ref, cols, kn[:, cols] + kr, k_tabs, 1.0))
    sums = [jnp.sum(x * x, axis=-1, keepdims=True) for _, _, x, _, _ in tiles]
    rs = [lax.rsqrt(ss * (1.0 / QK) + EPS) * sc for ss, (_, _, _, _, sc) in zip(sums, tiles)]
    rot = [x * tabs[0]
           + jnp.where(upper, pltpu.roll(x, ROPE // 2, 1), pltpu.roll(x, LANE - ROPE // 2, 1)) * tabs[1]
           for _, _, x, tabs, _ in tiles]
    for (ref, cols, _, _, _), y, r in zip(tiles, rot, rs):
        ref[:, cols] = (y * r).astype(ref.dtype)


def _mla_prep(proj, ctab, stab, gq, gkv, wq, wk, wv, qg, kg, tm=512):
    n = proj.shape[0]
    wm = 6 * LANE
    full = lambda shp: pl.BlockSpec(shp, lambda i: (0,) * len(shp))
    return pl.pallas_call(
        _mla_prep_kernel,
        grid=(n // tm,),
        in_specs=[pl.BlockSpec((tm, wm), lambda i: (i, 0)),
                  pl.BlockSpec((tm, LANE), lambda i: (i, 0)),
                  pl.BlockSpec((tm, LANE), lambda i: (i, 0)),
                  full((1, Q_LORA)), full((1, KV_LORA)),
                  full((Q_LORA, H * LANE)), full((KV_LORA, H * LANE)), full((KV_LORA, H * LANE)),
                  full((2, LANE)), full((2, LANE))],
        out_specs=[pl.BlockSpec((tm, H * LANE), lambda i: (i, 0))] * 3,
        out_shape=[jax.ShapeDtypeStruct((n, H * LANE), BF16)] * 3,
        compiler_params=_cparams(("parallel",)),
        name="mla_prep",
    )(proj, ctab, stab, gq, gkv, wq, wk, wv, qg, kg)


def _attn_kernel(q_ref, k_ref, v_ref, o_ref, *, t):
    qi = pl.program_id(2)
    heads = (slice(0, LANE), slice(LANE, 2 * LANE))
    qs = [q_ref[:, c] for c in heads]

    def step(start, w, carry, masked):
        ss = [lax.dot_general(q, k_ref[pl.ds(start, w), c], (((1,), (1,)), ((), ())),
                              preferred_element_type=F32) for c, q in zip(heads, qs)]
        if masked:
            row = lax.broadcasted_iota(jnp.int32, (t, w), 0)
            col = lax.broadcasted_iota(jnp.int32, (t, w), 1)
            ok = col <= row + (w - t)
            ss = [jnp.where(ok, s, -jnp.inf) for s in ss]
        ms = [jnp.maximum(m, jnp.max(s, axis=-1, keepdims=True)) for s, (m, _) in zip(ss, carry)]
        ps = [jnp.exp2(s - m_new).astype(BF16) for s, m_new in zip(ss, ms)]
        pv = [jnp.dot(p, v_ref[pl.ds(start, w), c], preferred_element_type=F32) for p, c in zip(ps, heads)]
        return tuple((m_new, jnp.exp2(m - m_new) * acc + x) for m_new, (m, acc), x in zip(ms, carry, pv))

    init = tuple((jnp.full((t, 1), -jnp.inf, F32), jnp.zeros((t, LANE), F32)) for _ in heads)
    carry = lax.fori_loop(
        0, qi // 2, lambda j, c: step(pl.multiple_of(j * (2 * t), 2 * t), 2 * t, c, False), init)
    (_, acc0), (_, acc1) = lax.cond(
        qi % 2 == 1,
        lambda c: step(pl.multiple_of((qi - 1) * t, 2 * t), 2 * t, c, True),
        lambda c: step(pl.multiple_of(qi * t, t), t, c, True),
        carry)
    out0 = acc0 / acc0[:, VD:VD + 1]
    out1 = acc1 / acc1[:, VD:VD + 1]
    lane = lax.broadcasted_iota(jnp.int32, (t, LANE), 1)
    o_ref[...] = jnp.where(lane < VD, out0, pltpu.roll(out1, VD, 1)).astype(o_ref.dtype)


def _attention(q, k, v, batch, seq, t=1024):
    n = q.shape[0]
    nq = seq // t
    k3 = k.reshape(batch, seq, H * LANE)
    v3 = v.reshape(batch, seq, H * LANE)
    return pl.pallas_call(
        functools.partial(_attn_kernel, t=t),
        grid=(batch, H // 2, nq),
        in_specs=[pl.BlockSpec((t, 2 * LANE), lambda b, hp, i: (b * nq + i, hp)),
                  pl.BlockSpec((None, seq, 2 * LANE), lambda b, hp, i: (b, 0, hp), pipeline_mode=pl.Buffered(1)),
                  pl.BlockSpec((None, seq, 2 * LANE), lambda b, hp, i: (b, 0, hp), pipeline_mode=pl.Buffered(1))],
        out_specs=pl.BlockSpec((t, 2 * VD), lambda b, hp, i: (b * nq + i, hp)),
        out_shape=jax.ShapeDtypeStruct((n, H * VD), BF16),
        compiler_params=_cparams(("parallel", "parallel", "arbitrary")),
        name="attention",
    )(q, k3, v3)


GW = 4 * DK


def _blockdiag(x, mask):
    return jnp.where(mask, jnp.concatenate([x] * (GW // CC), axis=0), jnp.zeros((), x.dtype))


def _split2(x):
    hi = x.astype(BF16)
    lo = (x - hi.astype(F32)).astype(BF16)
    return hi, lo


def _split3(x):
    hi = x.astype(BF16)
    r = x - hi.astype(F32)
    mid = r.astype(BF16)
    lo = (r - mid.astype(F32)).astype(BF16)
    return hi, mid, lo


def _dot_exact_rhs(sel, x):
    n = x.shape[1]
    r = jnp.dot(sel, jnp.concatenate(_split3(x), axis=1), preferred_element_type=F32)
    return r[:, :n] + r[:, n:2 * n] + r[:, 2 * n:]


def _dot3(a, b_hi, b_lo):
    a_hi, a_lo = _split2(a)
    m = a.shape[0]
    r = jnp.dot(jnp.concatenate([a_hi, a_lo], axis=0), b_hi, preferred_element_type=F32)
    return r[:m] + r[m:] + jnp.dot(a_hi, b_lo, preferred_element_type=F32)


def _segsum(x, e):
    hi, lo = _split2(x)
    m = x.shape[0]
    r = jnp.dot(jnp.concatenate([hi, lo], axis=0), e, preferred_element_type=F32)
    return r[:m] + r[m:]


def _deltanet_kernel(qkv_ref, z_ref, b_ref, a_ref, cw_ref, alog_ref, dtb_ref, og_ref, e_ref,
                     o_ref, xpad_ref, state_ref, *, tm):
    t_idx = pl.program_id(1)
    nch = tm // CC
    ng = (H * DK) // GW
    width = 3 * H * DK

    @pl.when(t_idx == 0)
    def _():
        xpad_ref[0:8, :] = jnp.zeros((8, width), F32)
        state_ref[...] = jnp.zeros(state_ref.shape, F32)

    @pl.when(t_idx != 0)
    def _():
        xpad_ref[0:8, :] = xpad_ref[tm:tm + 8, :]

    xpad_ref[8:8 + tm, :] = qkv_ref[...].astype(F32)

    xc = jnp.zeros((tm, width), F32)
    for j in range(CONV):
        xc = xc + cw_ref[j:j + 1, :] * xpad_ref[8 - (CONV - 1) + j: 8 - (CONV - 1) + j + tm, :]
    xc = xc * jax.nn.sigmoid(xc)
    e = e_ref[...]
    hw = H * DK
    q = xc[:, :hw]
    k = xc[:, hw:2 * hw]
    v = xc[:, 2 * hw:]
    q = q * lax.rsqrt(_segsum(q * q, e) + EPS) * (DK ** -0.5)
    k = k * lax.rsqrt(_segsum(k * k, e) + EPS)
    beta = jax.nn.sigmoid(b_ref[...].astype(F32))
    sp_in = a_ref[...].astype(F32) + dtb_ref[...]
    softplus = jnp.maximum(sp_in, 0.0) + jnp.log1p(jnp.exp(-jnp.abs(sp_in)))
    g = -jnp.exp(alog_ref[...]) * softplus

    ri = lax.broadcasted_iota(jnp.int32, (tm, tm), 0)
    ci = lax.broadcasted_iota(jnp.int32, (tm, tm), 1)
    cum = jnp.where((ri // CC == ci // CC) & (ci <= ri), 1.0, 0.0).astype(BF16)
    gc = _dot_exact_rhs(cum, g)
    egc = jnp.exp(gc)
    kb = k * beta
    vb = v * beta
    kbg = kb * egc
    qg = q * egc

    r64 = lax.broadcasted_iota(jnp.int32, (CC, GW), 0)
    l64 = lax.broadcasted_iota(jnp.int32, (CC, GW), 1) % CC
    tril = l64 <= r64
    strict = l64 < r64
    eye_t = jnp.where(l64 == r64, 1.0, 0.0).astype(F32)
    rb = lax.broadcasted_iota(jnp.int32, (GW, GW), 0) // CC
    lb = lax.broadcasted_iota(jnp.int32, (GW, GW), 1) // CC
    bd = rb == lb
    ones = jnp.ones((CC, CC), BF16)

    tiles = [(slice(ch * CC, (ch + 1) * CC), slice(gi * GW, (gi + 1) * GW), ch, gi)
             for ch in range(nch) for gi in range(ng)]
    low_l, intra_l = [], []
    for rows, cols, _, _ in tiles:
        gc_c = gc[rows, cols]
        bdk = _blockdiag(k[rows, cols].astype(BF16), bd)
        lhs = jnp.concatenate([kb[rows, cols], q[rows, cols]], axis=0).astype(BF16)
        a2 = lax.dot_general(lhs, bdk, (((1,), (1,)), ((), ())), preferred_element_type=F32)
        gct = _dot_exact_rhs(ones, jnp.where(l64 == r64, gc_c, 0.0))
        decay = jnp.exp(jnp.where(tril, gc_c - gct, -jnp.inf))
        low_l.append(jnp.where(strict, a2[:CC] * decay, 0.0))
        intra_l.append(jnp.where(tril, a2[CC:] * decay, 0.0).astype(BF16))
    p_l = [eye_t - low for low in low_l]
    m_l = [_dot3(low, *[_blockdiag(piece, bd) for piece in _split2(low)]) for low in low_l]
    n_sq = int(math.log2(CC)) - 1
    for lvl in range(n_sq):
        for i in range(len(tiles)):
            bdm = [_blockdiag(piece, bd) for piece in _split2(m_l[i])]
            if lvl < n_sq - 1:
                r = _dot3(jnp.concatenate([p_l[i], m_l[i]], axis=0), *bdm)
                p_l[i] = p_l[i] + r[:CC]
                m_l[i] = r[CC:]
            else:
                p_l[i] = p_l[i] + _dot3(p_l[i], *bdm)
    uw_l = []
    for (rows, cols, _, _), p_acc in zip(tiles, p_l):
        rhs = [jnp.concatenate([_blockdiag(a, bd), _blockdiag(b, bd)], axis=1)
               for a, b in zip(_split2(vb[rows, cols]), _split2(kbg[rows, cols]))]
        uw_l.append(_dot3(p_acc, *rhs))

    states = [state_ref[gi] for gi in range(ng)]
    o_l = []
    for (rows, cols, ch, gi), uw, intra in zip(tiles, uw_l, intra_l):
        st = states[gi]
        lhs2 = jnp.concatenate([uw[:, GW:], qg[rows, cols]], axis=0).astype(BF16)
        ws = jnp.dot(lhs2, st.astype(BF16), preferred_element_type=F32)
        v_new = (uw[:, :GW] - ws[:CC]).astype(BF16)
        o = ws[CC:] + jnp.dot(intra, _blockdiag(v_new, bd), preferred_element_type=F32)
        g_last = gc[ch * CC + CC - 1: ch * CC + CC, cols]
        kd = (k[rows, cols] * jnp.exp(g_last - gc[rows, cols])).astype(BF16)
        upd = lax.dot_general(kd, v_new, (((0,), (0,)), ((), ())), preferred_element_type=F32)
        states[gi] = st * jnp.exp(g_last) + jnp.where(bd, upd, 0.0)
        o_l.append(o)
    for gi in range(ng):
        state_ref[gi] = states[gi]

    o = jnp.concatenate([jnp.concatenate(o_l[ch * ng:(ch + 1) * ng], axis=1) for ch in range(nch)], axis=0)
    o = o * lax.rsqrt(_segsum(o * o, e) * (1.0 / DV) + EPS) * og_ref[...]
    z = z_ref[...].astype(F32)
    o_ref[...] = (o * (z * jax.nn.sigmoid(z))).astype(o_ref.dtype)


def _deltanet(proj, conv_w, alog, dtb, og, batch, seq, tm=256):
    n = proj.shape[0]
    nt = seq // tm
    hw = H * DK
    e = jnp.asarray(np.kron(np.eye(H), np.ones((DK, DK))), BF16)
    full = lambda shp: pl.BlockSpec(shp, lambda b, i: (0,) * len(shp))
    blk = lambda unit, w: pl.BlockSpec((tm, w), lambda b, i: (b * nt + i, unit * LANE // w))
    return pl.pallas_call(
        functools.partial(_deltanet_kernel, tm=tm),
        grid=(batch, nt),
        in_specs=[blk(U_QKV, 3 * hw), blk(U_CZ, hw), blk(U_CB, hw), blk(U_CA, hw),
                  full((CONV, 3 * hw)), full((1, hw)), full((1, hw)), full((1, hw)), full((hw, hw))],
        out_specs=pl.BlockSpec((tm, hw), lambda b, i: (b * nt + i, 0)),
        out_shape=jax.ShapeDtypeStruct((n, hw), BF16),
        scratch_shapes=[pltpu.VMEM((tm + 8, 3 * hw), F32),
                        pltpu.VMEM((hw // GW, GW, GW), F32)],
        compiler_params=_cparams(("parallel", "arbitrary")),
        name="deltanet",
    )(proj, proj, proj, proj, conv_w, alog, dtb, og, e)


def _mix_ffn_kernel(x_ref, ya_ref, yb_ref, yc_ref, g0_ref, g1_ref, g2_ref, wb_ref, wo_ref, gn_ref,
                    w1_ref, w2_ref, o_ref, a_ref, *, tf):
    merged = None
    for i, (y_ref, g_ref) in enumerate(((ya_ref, g0_ref), (yb_ref, g1_ref), (yc_ref, g2_ref))):
        y = jnp.dot(y_ref[...], wb_ref[i], preferred_element_type=F32)
        term = jax.nn.sigmoid(g_ref[...].astype(F32)) * y
        merged = term if merged is None else merged + term
    x1 = x_ref[...] + jnp.dot(merged.astype(BF16), wo_ref[...], preferred_element_type=F32)
    h = _rms(x1, gn_ref[...]).astype(BF16)
    for f in range(DFF // tf):
        cols = slice(f * tf, (f + 1) * tf)
        a = jnp.dot(h, w1_ref[:, cols], preferred_element_type=F32)
        a_ref[:, cols] = jnp.square(jnp.maximum(a, 0.0)).astype(BF16)
    o_ref[...] = x1 + jnp.dot(a_ref[...], w2_ref[...], preferred_element_type=F32)


def _mix_ffn(x, ya, yb, yc, proj, wb, wo, gn, w1, w2, tm=512, tf=1024):
    n = x.shape[0]
    tok = lambda w: pl.BlockSpec((tm, w), lambda r: (r, 0))
    gate = lambda i: pl.BlockSpec((tm, D), lambda r, i=i: (r, U_GATE // 8 + i))
    res = lambda shp: pl.BlockSpec(shp, lambda r: (0,) * len(shp), pipeline_mode=pl.Buffered(1))
    return pl.pallas_call(
        functools.partial(_mix_ffn_kernel, tf=tf),
        grid=(n // tm,),
        in_specs=[tok(D), tok(BW), tok(BW), tok(BW), gate(0), gate(1), gate(2),
                  res((NB, BW, D)), res((D, D)), res((1, D)), res((D, DFF)), res((DFF, D))],
        out_specs=tok(D),
        out_shape=jax.ShapeDtypeStruct((n, D), F32),
        scratch_shapes=[pltpu.VMEM((tm, DFF), BF16)],
        compiler_params=_cparams(("parallel",)),
        name="mix_ffn",
    )(x, ya, yb, yc, proj, proj, proj, wb, wo, gn, w1, w2)


def _pad_heads(w, per_head, used):
    kdim = w.shape[0]
    w = w.reshape(kdim, H, per_head)[:, :, :used]
    w = jnp.pad(w, ((0, 0), (0, 0), (0, LANE - used)))
    return w.reshape(kdim, H * LANE)


def kernel(x, positions, norm1_g, w_in, sgu_norm_g, w_spatial, b_spatial, q_lat_norm_g, w_q_up,
           kv_lat_norm_g, w_kv_up, q_norm_g, k_norm_g, conv_w, a_log, dt_bias, o_norm_g,
           w_branch, w_out, norm2_g, w_ff1, w_ff2):
    batch, seq, d = x.shape
    depth = w_in.shape[0]
    n = batch * seq
    xt = x.reshape(n, d)
    ctab, stab = _rope_tables(positions.reshape(n, 1))
    row = lambda a: a.reshape(1, -1).astype(F32)
    half = ROPE // 2
    for l in range(depth):
        proj = _inproj(xt, row(norm1_g[l]), _permute_w_in(w_in[l]))

        wcat = w_spatial[l].reshape(A_G // 2, 2, A_T, A_T).transpose(0, 2, 1, 3).reshape(A_G // 2, A_T, 2 * A_T)
        bias = jnp.repeat(b_spatial[l].T, A_W // A_G, axis=1).astype(F32)
        y_a = _sgu(proj, row(sgu_norm_g[l]), wcat.astype(F32), bias)

        wq = _pad_heads(w_q_up[l], QK, QK).astype(BF16)
        wkv = w_kv_up[l].reshape(KV_LORA, H, NOPE + VD)
        wk = jnp.pad(wkv[:, :, :NOPE], ((0, 0), (0, 0), (0, LANE - NOPE))).reshape(KV_LORA, H * LANE).astype(BF16)
        wv = jnp.pad(wkv[:, :, NOPE:], ((0, 0), (0, 0), (0, LANE - VD))).reshape(KV_LORA, H * LANE).astype(BF16)
        pad_g = lambda gvec: jnp.pad(
            jnp.stack([gvec, jnp.concatenate([gvec[:NOPE], gvec[NOPE + half:], gvec[NOPE:NOPE + half]])]).astype(F32),
            ((0, 0), (0, LANE - QK)))
        q, k, v = _mla_prep(proj, ctab, stab, row(q_lat_norm_g[l]), row(kv_lat_norm_g[l]),
                            wq, wk, wv, pad_g(q_norm_g[l]), pad_g(k_norm_g[l]))
        y_b = _attention(q, k, v, batch, seq)

        rep = lambda a: jnp.repeat(a.astype(F32), DK).reshape(1, H * DK)
        y_c = _deltanet(proj, conv_w[l].astype(F32), rep(a_log[l]), rep(dt_bias[l]),
                        jnp.tile(o_norm_g[l].astype(F32), H).reshape(1, H * DV), batch, seq)

        xt = _mix_ffn(xt, y_a, y_b, y_c, proj, w_branch[l].astype(BF16), w_out[l].astype(BF16),
                      row(norm2_g[l]), w_ff1[l].astype(BF16), w_ff2[l].astype(BF16))
    return xt.reshape(batch, seq, d)
```

```python
import functools
import math

import numpy as np
import jax
import jax.numpy as jnp
from jax import lax
from jax.experimental import pallas as pl
from jax.experimental.pallas import tpu as pltpu

F32 = jnp.float32
BF16 = jnp.bfloat16

LANE = 128
VMEM_LIMIT = 56 * 1024 * 1024

D = 1024
A_W = 512
A_G = 8
A_T = 128
H = 8
NOPE = 64
ROPE = 32
VD = 64
QK = NOPE + ROPE
Q_LORA = 384
KV_LORA = 256
ROPE_BASE = 10000.0
DK = 64
DV = 64
CONV = 4
CC = 64
NB = 3
BW = 512
DFF = 4 * D
EPS = 1e-6

U_MLA, U_CZ, U_QKV, U_AU, U_AV, U_CB, U_CA, U_GATE = 0, 8, 12, 24, 28, 32, 36, 40
DP = 64 * LANE

O_AU, O_AV, O_QLAT, O_KVLAT, O_KROPE = 0, 512, 1024, 1408, 1664
O_CQKV, O_CZ, O_CB, O_CA, O_GATES = 1696, 3232, 3744, 3752, 3760


def _permute_w_in(w):
    kdim = w.shape[0]
    sl = lambda off, n: w[:, off:off + n]
    zeros = lambda n: jnp.zeros((kdim, n), w.dtype)
    pieces = [
        sl(O_QLAT, Q_LORA), sl(O_KVLAT, KV_LORA),
        zeros(NOPE), sl(O_KROPE, ROPE), zeros(LANE - QK),
        zeros((U_CZ - U_MLA - 6) * LANE),
        sl(O_CZ, 512), sl(O_CQKV, 1536), sl(O_AU, A_W), sl(O_AV, A_W),
        jnp.repeat(sl(O_CB, H), DK, axis=1),
        jnp.repeat(sl(O_CA, H), DK, axis=1),
        sl(O_GATES, NB * D),
    ]
    out = jnp.concatenate(pieces, axis=1).astype(BF16)
    assert out.shape == (kdim, DP)
    return out


def _cparams(sem):
    return pltpu.CompilerParams(dimension_semantics=sem, vmem_limit_bytes=VMEM_LIMIT)


def _inproj_kernel(x_ref, g_ref, w_ref, o_ref, *, tn):
    x = x_ref[...]
    ms = jnp.mean(x * x, axis=-1, keepdims=True)
    h = (x * lax.rsqrt(ms + EPS) * g_ref[...]).astype(BF16)
    for j in range(w_ref.shape[1] // tn):
        cols = slice(j * tn, (j + 1) * tn)
        o_ref[:, cols] = jnp.dot(h, w_ref[:, cols], preferred_element_type=F32).astype(o_ref.dtype)


def _inproj(x, g, w, tm=512, tn=1024):
    n = x.shape[0]
    dp = w.shape[1]
    return pl.pallas_call(
        functools.partial(_inproj_kernel, tn=tn),
        grid=(n // tm,),
        in_specs=[pl.BlockSpec((tm, D), lambda i: (i, 0)),
                  pl.BlockSpec((1, D), lambda i: (0, 0)),
                  pl.BlockSpec((D, dp), lambda i: (0, 0), pipeline_mode=pl.Buffered(1))],
        out_specs=pl.BlockSpec((tm, dp), lambda i: (i, 0)),
        out_shape=jax.ShapeDtypeStruct((n, dp), BF16),
        compiler_params=_cparams(("parallel",)),
        name="inproj",
    )(x, g, w)


def _rope_kernel(pos_ref, invf_ref, c_ref, s_ref):
    pos = pos_ref[...].astype(F32)
    ang = pos * invf_ref[...]
    lane = lax.broadcasted_iota(jnp.int32, ang.shape, 1)
    c = jnp.cos(ang)
    s = jnp.sin(ang)
    half = ROPE // 2
    c_ref[...] = jnp.where((lane >= NOPE) & (lane < QK), c, 1.0)
    s_ref[...] = jnp.where(lane < NOPE, 0.0,
                           jnp.where(lane < NOPE + half, -s, jnp.where(lane < QK, s, 0.0)))


def _rope_tables(pos, tm=1024):
    n = pos.shape[0]
    half = ROPE // 2
    lane = np.arange(LANE)
    invf = np.where((lane >= NOPE) & (lane < QK),
                    1.0 / (ROPE_BASE ** (((lane - NOPE) % half).astype(np.float32) / half)), 0.0)
    invf = jnp.asarray(invf, F32).reshape(1, LANE)
    return pl.pallas_call(
        _rope_kernel,
        grid=(n // tm,),
        in_specs=[pl.BlockSpec((tm, 1), lambda i: (i, 0)),
                  pl.BlockSpec((1, LANE), lambda i: (0, 0))],
        out_specs=[pl.BlockSpec((tm, LANE), lambda i: (i, 0)),
                   pl.BlockSpec((tm, LANE), lambda i: (i, 0))],
        out_shape=[jax.ShapeDtypeStruct((n, LANE), F32)] * 2,
        compiler_params=_cparams(("parallel",)),
        name="rope_tables",
    )(pos, invf)


def _gelu(x):
    return 0.5 * x * (1.0 + lax.erf(x * (1.0 / math.sqrt(2.0))))


def _sgu_kernel(u_ref, v_ref, g_ref, w_ref, b_ref, o_ref):
    tm = u_ref.shape[0]
    u = _gelu(u_ref[...].astype(F32))
    v = _gelu(v_ref[...].astype(F32))
    ms = jnp.mean(v * v, axis=-1, keepdims=True)
    v = v * lax.rsqrt(ms + EPS) * g_ref[...]
    lo = (lax.broadcasted_iota(jnp.int32, v.shape, 1) % LANE) < (A_W // A_G)
    v_lo = jnp.where(lo, v, 0.0).astype(BF16)
    v_hi = jnp.where(lo, 0.0, v).astype(BF16)
    r = lax.broadcasted_iota(jnp.int32, (A_T, 2 * A_T), 0)
    c = lax.broadcasted_iota(jnp.int32, (A_T, 2 * A_T), 1)
    causal = (c % A_T) <= r
    for p in range(A_G // 2):
        cols = slice(p * LANE, (p + 1) * LANE)
        w = jnp.where(causal, w_ref[p], 0.0).astype(BF16)
        for ch in range(tm // A_T):
            rows = slice(ch * A_T, (ch + 1) * A_T)
            rhs = jnp.concatenate([v_lo[rows, cols], v_hi[rows, cols]], axis=0)
            sv = jnp.dot(w, rhs, preferred_element_type=F32) + b_ref[:, cols]
            o_ref[rows, cols] = (u[rows, cols] * sv).astype(o_ref.dtype)


def _sgu(proj, g, wcat, bias, tm=512):
    n = proj.shape[0]
    return pl.pallas_call(
        _sgu_kernel,
        grid=(n // tm,),
        in_specs=[pl.BlockSpec((tm, A_W), lambda i: (i, U_AU // 4)),
                  pl.BlockSpec((tm, A_W), lambda i: (i, U_AV // 4)),
                  pl.BlockSpec((1, A_W), lambda i: (0, 0)),
                  pl.BlockSpec((A_G // 2, A_T, 2 * A_T), lambda i: (0, 0, 0)),
                  pl.BlockSpec((A_T, A_W), lambda i: (0, 0))],
        out_specs=pl.BlockSpec((tm, A_W), lambda i: (i, 0)),
        out_shape=jax.ShapeDtypeStruct((n, A_W), BF16),
        compiler_params=_cparams(("parallel",)),
        name="sgu",
    )(proj, proj, g, wcat, bias)


def _rms(x, g):
    ms = jnp.mean(x * x, axis=-1, keepdims=True)
    return x * lax.rsqrt(ms + EPS) * g


def _mla_prep_kernel(p_ref, c_ref, s_ref, gq_ref, gkv_ref, wq_ref, wk_ref, wv_ref, qg_ref, kg_ref,
                     q_ref, k_ref, v_ref):
    p = p_ref[...].astype(F32)
    ql = p[:, :Q_LORA]
    kvl = p[:, Q_LORA:Q_LORA + KV_LORA]
    kr = p[:, Q_LORA + KV_LORA:]
    qn = _rms(ql, gq_ref[...]).astype(BF16)
    kvn = _rms(kvl, gkv_ref[...]).astype(BF16)
    q = jnp.dot(qn, wq_ref[...], preferred_element_type=F32)
    kn = jnp.dot(kvn, wk_ref[...], preferred_element_type=F32)
    v = jnp.dot(kvn, wv_ref[...], preferred_element_type=F32)
    vlane = lax.broadcasted_iota(jnp.int32, v.shape, 1) % LANE
    v_ref[...] = jnp.where(vlane == VD, 1.0, v).astype(v_ref.dtype)
    cs = c_ref[...]
    sn = s_ref[...]
    lane = lax.broadcasted_iota(jnp.int32, cs.shape, 1)
    upper = lane >= NOPE + ROPE // 2
    scale = QK ** -0.5 * math.log2(math.e)

    q_tabs = (cs * qg_ref[0:1, :], sn * qg_ref[1:2, :])
    k_tabs = (cs * kg_ref[0:1, :], sn * kg_ref[1:2, :])

    tiles = []
    for h in range(H):
        cols = slice(h * LANE, (h + 1) * LANE)
        tiles.append((q_ref, cols, q[:, cols], q_tabs, scale))
        tiles.append((k_ref, cols, kn[:, cols] + kr, k_tabs, 1.0))
    sums = [jnp.sum(x * x, axis=-1, keepdims=True) for _, _, x, _, _ in tiles]
    rs = [lax.rsqrt(ss * (1.0 / QK) + EPS) * sc for ss, (_, _, _, _, sc) in zip(sums, tiles)]
    rot = [x * tabs[0]
           + jnp.where(upper, pltpu.roll(x, ROPE // 2, 1), pltpu.roll(x, LANE - ROPE // 2, 1)) * tabs[1]
           for _, _, x, tabs, _ in tiles]
    for (ref, cols, _, _, _), y, r in zip(tiles, rot, rs):
        ref[:, cols] = (y * r).astype(ref.dtype)


def _mla_prep(proj, ctab, stab, gq, gkv, wq, wk, wv, qg, kg, tm=512):
    n = proj.shape[0]
    wm = 6 * LANE
    full = lambda shp: pl.BlockSpec(shp, lambda i: (0,) * len(shp))
    return pl.pallas_call(
        _mla_prep_kernel,
        grid=(n // tm,),
        in_specs=[pl.BlockSpec((tm, wm), lambda i: (i, 0)),
                  pl.BlockSpec((tm, LANE), lambda i: (i, 0)),
                  pl.BlockSpec((tm, LANE), lambda i: (i, 0)),
                  full((1, Q_LORA)), full((1, KV_LORA)),
                  full((Q_LORA, H * LANE)), full((KV_LORA, H * LANE)), full((KV_LORA, H * LANE)),
                  full((2, LANE)), full((2, LANE))],
        out_specs=[pl.BlockSpec((tm, H * LANE), lambda i: (i, 0))] * 3,
        out_shape=[jax.ShapeDtypeStruct((n, H * LANE), BF16)] * 3,
        compiler_params=_cparams(("parallel",)),
        name="mla_prep",
    )(proj, ctab, stab, gq, gkv, wq, wk, wv, qg, kg)


def _attn_kernel(q_ref, k_ref, v_ref, o_ref, *, t):
    qi = pl.program_id(2)
    heads = (slice(0, LANE), slice(LANE, 2 * LANE))
    qs = [q_ref[:, c] for c in heads]

    def step(start, w, carry, masked):
        ss = [lax.dot_general(q, k_ref[pl.ds(start, w), c], (((1,), (1,)), ((), ())),
                              preferred_element_type=F32) for c, q in zip(heads, qs)]
        if masked:
            row = lax.broadcasted_iota(jnp.int32, (t, w), 0)
            col = lax.broadcasted_iota(jnp.int32, (t, w), 1)
            ok = col <= row + (w - t)
            ss = [jnp.where(ok, s, -jnp.inf) for s in ss]
        ms = [jnp.maximum(m, jnp.max(s, axis=-1, keepdims=True)) for s, (m, _) in zip(ss, carry)]
        ps = [jnp.exp2(s - m_new).astype(BF16) for s, m_new in zip(ss, ms)]
        pv = [jnp.dot(p, v_ref[pl.ds(start, w), c], preferred_element_type=F32) for p, c in zip(ps, heads)]
        return tuple((m_new, jnp.exp2(m - m_new) * acc + x) for m_new, (m, acc), x in zip(ms, carry, pv))

    init = tuple((jnp.full((t, 1), -jnp.inf, F32), jnp.zeros((t, LANE), F32)) for _ in heads)
    carry = lax.fori_loop(
        0, qi // 2, lambda j, c: step(pl.multiple_of(j * (2 * t), 2 * t), 2 * t, c, False), init)
    (_, acc0), (_, acc1) = lax.cond(
        qi % 2 == 1,
        lambda c: step(pl.multiple_of((qi - 1) * t, 2 * t), 2 * t, c, True),
        lambda c: step(pl.multiple_of(qi * t, t), t, c, True),
        carry)
    out0 = acc0 / acc0[:, VD:VD + 1]
    out1 = acc1 / acc1[:, VD:VD + 1]
    lane = lax.broadcasted_iota(jnp.int32, (t, LANE), 1)
    o_ref[...] = jnp.where(lane < VD, out0, pltpu.roll(out1, VD, 1)).astype(o_ref.dtype)


def _attention(q, k, v, batch, seq, t=1024):
    n = q.shape[0]
    nq = seq // t
    k3 = k.reshape(batch, seq, H * LANE)
    v3 = v.reshape(batch, seq, H * LANE)
    return pl.pallas_call(
        functools.partial(_attn_kernel, t=t),
        grid=(batch, H // 2, nq),
        in_specs=[pl.BlockSpec((t, 2 * LANE), lambda b, hp, i: (b * nq + i, hp)),
                  pl.BlockSpec((None, seq, 2 * LANE), lambda b, hp, i: (b, 0, hp), pipeline_mode=pl.Buffered(1)),
                  pl.BlockSpec((None, seq, 2 * LANE), lambda b, hp, i: (b, 0, hp), pipeline_mode=pl.Buffered(1))],
        out_specs=pl.BlockSpec((t, 2 * VD), lambda b, hp, i: (b * nq + i, hp)),
        out_shape=jax.ShapeDtypeStruct((n, H * VD), BF16),
        compiler_params=_cparams(("parallel", "parallel", "arbitrary")),
        name="attention",
    )(q, k3, v3)


GW = 4 * DK


def _blockdiag(x, mask):
    return jnp.where(mask, jnp.concatenate([x] * (GW // CC), axis=0), jnp.zeros((), x.dtype))


def _split3(x):
    hi = x.astype(BF16)
    r = x - hi.astype(F32)
    mid = r.astype(BF16)
    lo = (r - mid.astype(F32)).astype(BF16)
    return hi, mid, lo


def _dot_exact_rhs(sel, x):
    n = x.shape[1]
    r = jnp.dot(sel, jnp.concatenate(_split3(x), axis=1), preferred_element_type=F32)
    return r[:, :n] + r[:, n:2 * n] + r[:, 2 * n:]


def _bd_dot(a, b, mask):
    return jnp.dot(a.astype(BF16), _blockdiag(b.astype(BF16), mask), preferred_element_type=F32)


def _segsum(x, e):
    return jnp.dot(x.astype(BF16), e, preferred_element_type=F32)


def _deltanet_kernel(qkv_ref, z_ref, b_ref, a_ref, cw_ref, alog_ref, dtb_ref, og_ref, e_ref,
                     o_ref, xpad_ref, state_ref, *, tm):
    t_idx = pl.program_id(1)
    nch = tm // CC
    ng = (H * DK) // GW
    width = 3 * H * DK

    @pl.when(t_idx == 0)
    def _():
        xpad_ref[0:8, :] = jnp.zeros((8, width), F32)
        state_ref[...] = jnp.zeros(state_ref.shape, F32)

    @pl.when(t_idx != 0)
    def _():
        xpad_ref[0:8, :] = xpad_ref[tm:tm + 8, :]

    xpad_ref[8:8 + tm, :] = qkv_ref[...].astype(F32)

    xc = jnp.zeros((tm, width), F32)
    for j in range(CONV):
        xc = xc + cw_ref[j:j + 1, :] * xpad_ref[8 - (CONV - 1) + j: 8 - (CONV - 1) + j + tm, :]
    xc = xc * jax.nn.sigmoid(xc)
    e = e_ref[...]
    hw = H * DK
    q = xc[:, :hw]
    k = xc[:, hw:2 * hw]
    v = xc[:, 2 * hw:]
    q = q * lax.rsqrt(_segsum(q * q, e) + EPS) * (DK ** -0.5)
    k = k * lax.rsqrt(_segsum(k * k, e) + EPS)
    beta = jax.nn.sigmoid(b_ref[...].astype(F32))
    sp_in = a_ref[...].astype(F32) + dtb_ref[...]
    softplus = jnp.maximum(sp_in, 0.0) + jnp.log1p(jnp.exp(-jnp.abs(sp_in)))
    g = -jnp.exp(alog_ref[...]) * softplus

    ri = lax.broadcasted_iota(jnp.int32, (tm, tm), 0)
    ci = lax.broadcasted_iota(jnp.int32, (tm, tm), 1)
    cum = jnp.where((ri // CC == ci // CC) & (ci <= ri), 1.0, 0.0).astype(BF16)
    gc = _dot_exact_rhs(cum, g)
    egc = jnp.exp(gc)
    kb = k * beta
    vb = v * beta
    kbg = kb * egc
    qg = q * egc

    r64 = lax.broadcasted_iota(jnp.int32, (CC, GW), 0)
    l64 = lax.broadcasted_iota(jnp.int32, (CC, GW), 1) % CC
    tril = l64 <= r64
    strict = l64 < r64
    eye_t = jnp.where(l64 == r64, 1.0, 0.0).astype(F32)
    rb = lax.broadcasted_iota(jnp.int32, (GW, GW), 0) // CC
    lb = lax.broadcasted_iota(jnp.int32, (GW, GW), 1) // CC
    bd = rb == lb
    ones = jnp.ones((CC, CC), BF16)

    tiles = [(slice(ch * CC, (ch + 1) * CC), slice(gi * GW, (gi + 1) * GW), ch, gi)
             for ch in range(nch) for gi in range(ng)]
    low_l, intra_l = [], []
    for rows, cols, _, _ in tiles:
        gc_c = gc[rows, cols]
        bdk = _blockdiag(k[rows, cols].astype(BF16), bd)
        lhs = jnp.concatenate([kb[rows, cols], q[rows, cols]], axis=0).astype(BF16)
        a2 = lax.dot_general(lhs, bdk, (((1,), (1,)), ((), ())), preferred_element_type=F32)
        gct = _dot_exact_rhs(ones, jnp.where(l64 == r64, gc_c, 0.0))
        decay = jnp.exp(jnp.where(tril, gc_c - gct, -jnp.inf))
        low_l.append(jnp.where(strict, a2[:CC] * decay, 0.0))
        intra_l.append(jnp.where(tril, a2[CC:] * decay, 0.0).astype(BF16))
    p_l = [eye_t - low for low in low_l]
    m_l = [_bd_dot(low, low, bd) for low in low_l]
    n_sq = int(math.log2(CC)) - 1
    for lvl in range(n_sq):
        for i in range(len(tiles)):
            if lvl < n_sq - 1:
                r = _bd_dot(jnp.concatenate([p_l[i], m_l[i]], axis=0), m_l[i], bd)
                p_l[i] = p_l[i] + r[:CC]
                m_l[i] = r[CC:]
            else:
                p_l[i] = p_l[i] + _bd_dot(p_l[i], m_l[i], bd)
    uw_l = []
    for (rows, cols, _, _), p_acc in zip(tiles, p_l):
        rhs = jnp.concatenate([_blockdiag(vb[rows, cols].astype(BF16), bd),
                               _blockdiag(kbg[rows, cols].astype(BF16), bd)], axis=1)
        uw_l.append(jnp.dot(p_acc.astype(BF16), rhs, preferred_element_type=F32))

    states = [state_ref[gi] for gi in range(ng)]
    o_l = []
    for (rows, cols, ch, gi), uw, intra in zip(tiles, uw_l, intra_l):
        st = states[gi]
        lhs2 = jnp.concatenate([uw[:, GW:], qg[rows, cols]], axis=0).astype(BF16)
        ws = jnp.dot(lhs2, st.astype(BF16), preferred_element_type=F32)
        v_new = (uw[:, :GW] - ws[:CC]).astype(BF16)
        o = ws[CC:] + jnp.dot(intra, _blockdiag(v_new, bd), preferred_element_type=F32)
        g_last = gc[ch * CC + CC - 1: ch * CC + CC, cols]
        kd = (k[rows, cols] * jnp.exp(g_last - gc[rows, cols])).astype(BF16)
        upd = lax.dot_general(kd, v_new, (((0,), (0,)), ((), ())), preferred_element_type=F32)
        states[gi] = st * jnp.exp(g_last) + jnp.where(bd, upd, 0.0)
        o_l.append(o)
    for gi in range(ng):
        state_ref[gi] = states[gi]

    o = jnp.concatenate([jnp.concatenate(o_l[ch * ng:(ch + 1) * ng], axis=1) for ch in range(nch)], axis=0)
    o = o * lax.rsqrt(_segsum(o * o, e) * (1.0 / DV) + EPS) * og_ref[...]
    z = z_ref[...].astype(F32)
    o_ref[...] = (o * (z * jax.nn.sigmoid(z))).astype(o_ref.dtype)


def _deltanet(proj, conv_w, alog, dtb, og, batch, seq, tm=256):
    n = proj.shape[0]
    nt = seq // tm
    hw = H * DK
    e = jnp.asarray(np.kron(np.eye(H), np.ones((DK, DK))), BF16)
    full = lambda shp: pl.BlockSpec(shp, lambda b, i: (0,) * len(shp))
    blk = lambda unit, w: pl.BlockSpec((tm, w), lambda b, i: (b * nt + i, unit * LANE // w))
    return pl.pallas_call(
        functools.partial(_deltanet_kernel, tm=tm),
        grid=(batch, nt),
        in_specs=[blk(U_QKV, 3 * hw), blk(U_CZ, hw), blk(U_CB, hw), blk(U_CA, hw),
                  full((CONV, 3 * hw)), full((1, hw)), full((1, hw)), full((1, hw)), full((hw, hw))],
        out_specs=pl.BlockSpec((tm, hw), lambda b, i: (b * nt + i, 0)),
        out_shape=jax.ShapeDtypeStruct((n, hw), BF16),
        scratch_shapes=[pltpu.VMEM((tm + 8, 3 * hw), F32),
                        pltpu.VMEM((hw // GW, GW, GW), F32)],
        compiler_params=_cparams(("parallel", "arbitrary")),
        name="deltanet",
    )(proj, proj, proj, proj, conv_w, alog, dtb, og, e)


def _mix_ffn_kernel(x_ref, ya_ref, yb_ref, yc_ref, g0_ref, g1_ref, g2_ref, wb_ref, wo_ref, gn_ref,
                    w1_ref, w2_ref, o_ref, a_ref, *, tf):
    merged = None
    for i, (y_ref, g_ref) in enumerate(((ya_ref, g0_ref), (yb_ref, g1_ref), (yc_ref, g2_ref))):
        y = jnp.dot(y_ref[...], wb_ref[i], preferred_element_type=F32)
        term = jax.nn.sigmoid(g_ref[...].astype(F32)) * y
        merged = term if merged is None else merged + term
    x1 = x_ref[...] + jnp.dot(merged.astype(BF16), wo_ref[...], preferred_element_type=F32)
    h = _rms(x1, gn_ref[...]).astype(BF16)
    for f in range(DFF // tf):
        cols = slice(f * tf, (f + 1) * tf)
        a = jnp.dot(h, w1_ref[:, cols], preferred_element_type=F32)
        a_ref[:, cols] = jnp.square(jnp.maximum(a, 0.0)).astype(BF16)
    o_ref[...] = x1 + jnp.dot(a_ref[...], w2_ref[...], preferred_element_type=F32)


def _mix_ffn(x, ya, yb, yc, proj, wb, wo, gn, w1, w2, tm=512, tf=1024):
    n = x.shape[0]
    tok = lambda w: pl.BlockSpec((tm, w), lambda r: (r, 0))
    gate = lambda i: pl.BlockSpec((tm, D), lambda r, i=i: (r, U_GATE // 8 + i))
    res = lambda shp: pl.BlockSpec(shp, lambda r: (0,) * len(shp), pipeline_mode=pl.Buffered(1))
    return pl.pallas_call(
        functools.partial(_mix_ffn_kernel, tf=tf),
        grid=(n // tm,),
        in_specs=[tok(D), tok(BW), tok(BW), tok(BW), gate(0), gate(1), gate(2),
                  res((NB, BW, D)), res((D, D)), res((1, D)), res((D, DFF)), res((DFF, D))],
        out_specs=tok(D),
        out_shape=jax.ShapeDtypeStruct((n, D), F32),
        scratch_shapes=[pltpu.VMEM((tm, DFF), BF16)],
        compiler_params=_cparams(("parallel",)),
        name="mix_ffn",
    )(x, ya, yb, yc, proj, proj, proj, wb, wo, gn, w1, w2)


def _pad_heads(w, per_head, used):
    kdim = w.shape[0]
    w = w.reshape(kdim, H, per_head)[:, :, :used]
    w = jnp.pad(w, ((0, 0), (0, 0), (0, LANE - used)))
    return w.reshape(kdim, H * LANE)


def kernel(x, positions, norm1_g, w_in, sgu_norm_g, w_spatial, b_spatial, q_lat_norm_g, w_q_up,
           kv_lat_norm_g, w_kv_up, q_norm_g, k_norm_g, conv_w, a_log, dt_bias, o_norm_g,
           w_branch, w_out, norm2_g, w_ff1, w_ff2):
    batch, seq, d = x.shape
    depth = w_in.shape[0]
    n = batch * seq
    xt = x.reshape(n, d)
    ctab, stab = _rope_tables(positions.reshape(n, 1))
    row = lambda a: a.reshape(1, -1).astype(F32)
    half = ROPE // 2
    for l in range(depth):
        proj = _inproj(xt, row(norm1_g[l]), _permute_w_in(w_in[l]))

        wcat = w_spatial[l].reshape(A_G // 2, 2, A_T, A_T).transpose(0, 2, 1, 3).reshape(A_G // 2, A_T, 2 * A_T)
        bias = jnp.repeat(b_spatial[l].T, A_W // A_G, axis=1).astype(F32)
        y_a = _sgu(proj, row(sgu_norm_g[l]), wcat.astype(F32), bias)

        wq = _pad_heads(w_q_up[l], QK, QK).astype(BF16)
        wkv = w_kv_up[l].reshape(KV_LORA, H, NOPE + VD)
        wk = jnp.pad(wkv[:, :, :NOPE], ((0, 0), (0, 0), (0, LANE - NOPE))).reshape(KV_LORA, H * LANE).astype(BF16)
        wv = jnp.pad(wkv[:, :, NOPE:], ((0, 0), (0, 0), (0, LANE - VD))).reshape(KV_LORA, H * LANE).astype(BF16)
        pad_g = lambda gvec: jnp.pad(
            jnp.stack([gvec, jnp.concatenate([gvec[:NOPE], gvec[NOPE + half:], gvec[NOPE:NOPE + half]])]).astype(F32),
            ((0, 0), (0, LANE - QK)))
        q, k, v = _mla_prep(proj, ctab, stab, row(q_lat_norm_g[l]), row(kv_lat_norm_g[l]),
                            wq, wk, wv, pad_g(q_norm_g[l]), pad_g(k_norm_g[l]))
        y_b = _attention(q, k, v, batch, seq)

        rep = lambda a: jnp.repeat(a.astype(F32), DK).reshape(1, H * DK)
        y_c = _deltanet(proj, conv_w[l].astype(F32), rep(a_log[l]), rep(dt_bias[l]),
                        jnp.tile(o_norm_g[l].astype(F32), H).reshape(1, H * DV), batch, seq)

        xt = _mix_ffn(xt, y_a, y_b, y_c, proj, w_branch[l].astype(BF16), w_out[l].astype(BF16),
                      row(norm2_g[l]), w_ff1[l].astype(BF16), w_ff2[l].astype(BF16))
    return xt.reshape(batch, seq, d)
```

```python
import functools
import math

import numpy as np
import jax
import jax.numpy as jnp
from jax import lax
from jax.experimental import pallas as pl
from jax.experimental.pallas import tpu as pltpu

F32 = jnp.float32
BF16 = jnp.bfloat16

LANE = 128
VMEM_LIMIT = 56 * 1024 * 1024

D = 1024
A_W = 512
A_G = 8
A_T = 128
H = 8
NOPE = 64
ROPE = 32
VD = 64
QK = NOPE + ROPE
Q_LORA = 384
KV_LORA = 256
ROPE_BASE = 10000.0
DK = 64
DV = 64
CONV = 4
CC = 64
SB = 16
NB = 3
BW = 512
DFF = 4 * D
EPS = 1e-6

U_MLA, U_CZ, U_QKV, U_AU, U_AV, U_CB, U_CA, U_GATE = 0, 8, 12, 24, 28, 32, 36, 40
DP = 64 * LANE

O_AU, O_AV, O_QLAT, O_KVLAT, O_KROPE = 0, 512, 1024, 1408, 1664
O_CQKV, O_CZ, O_CB, O_CA, O_GATES = 1696, 3232, 3744, 3752, 3760


def _permute_w_in(w):
    sl = lambda off, n: w[..., off:off + n]
    zeros = lambda n: jnp.zeros(w.shape[:-1] + (n,), w.dtype)
    pieces = [
        sl(O_QLAT, Q_LORA), sl(O_KVLAT, KV_LORA),
        zeros(NOPE), sl(O_KROPE, ROPE), zeros(LANE - QK),
        zeros((U_CZ - U_MLA - 6) * LANE),
        sl(O_CZ, 512), sl(O_CQKV, 1536), sl(O_AU, A_W), sl(O_AV, A_W),
        jnp.repeat(sl(O_CB, H), DK, axis=-1),
        jnp.repeat(sl(O_CA, H), DK, axis=-1),
        sl(O_GATES, NB * D),
    ]
    out = jnp.concatenate(pieces, axis=-1).astype(BF16)
    assert out.shape[-1] == DP
    return out


def _cparams(sem):
    return pltpu.CompilerParams(dimension_semantics=sem, vmem_limit_bytes=VMEM_LIMIT)


def _layer_spec(shape, layer, single_buffer=False):
    index_map = lambda *_: (layer,) + (0,) * len(shape)
    if single_buffer:
        return pl.BlockSpec((None,) + shape, index_map, pipeline_mode=pl.Buffered(1))
    return pl.BlockSpec((None,) + shape, index_map)


def _inproj_kernel(x_ref, g_ref, w_ref, o_ref, *, tn):
    x = x_ref[...]
    ms = jnp.mean(x * x, axis=-1, keepdims=True)
    h = (x * lax.rsqrt(ms + EPS) * g_ref[...]).astype(BF16)
    for j in range(w_ref.shape[1] // tn):
        cols = slice(j * tn, (j + 1) * tn)
        o_ref[:, cols] = jnp.dot(h, w_ref[:, cols], preferred_element_type=F32).astype(o_ref.dtype)


def _inproj(x, g, w, layer, tm=512, tn=1024):
    n = x.shape[0]
    dp = w.shape[-1]
    return pl.pallas_call(
        functools.partial(_inproj_kernel, tn=tn),
        grid=(n // tm,),
        in_specs=[pl.BlockSpec((tm, D), lambda i: (i, 0)),
                  _layer_spec((1, D), layer),
                  _layer_spec((D, dp), layer, single_buffer=True)],
        out_specs=pl.BlockSpec((tm, dp), lambda i: (i, 0)),
        out_shape=jax.ShapeDtypeStruct((n, dp), BF16),
        compiler_params=_cparams(("parallel",)),
        name="inproj",
    )(x, g, w)


def _rope_kernel(pos_ref, invf_ref, c_ref, s_ref):
    pos = pos_ref[...].astype(F32)
    ang = pos * invf_ref[...]
    lane = lax.broadcasted_iota(jnp.int32, ang.shape, 1)
    c = jnp.cos(ang)
    s = jnp.sin(ang)
    half = ROPE // 2
    c_ref[...] = jnp.where((lane >= NOPE) & (lane < QK), c, 1.0)
    s_ref[...] = jnp.where(lane < NOPE, 0.0,
                           jnp.where(lane < NOPE + half, -s, jnp.where(lane < QK, s, 0.0)))


def _rope_tables(pos, tm=1024):
    n = pos.shape[0]
    half = ROPE // 2
    lane = np.arange(LANE)
    invf = np.where((lane >= NOPE) & (lane < QK),
                    1.0 / (ROPE_BASE ** (((lane - NOPE) % half).astype(np.float32) / half)), 0.0)
    invf = jnp.asarray(invf, F32).reshape(1, LANE)
    return pl.pallas_call(
        _rope_kernel,
        grid=(n // tm,),
        in_specs=[pl.BlockSpec((tm, 1), lambda i: (i, 0)),
                  pl.BlockSpec((1, LANE), lambda i: (0, 0))],
        out_specs=[pl.BlockSpec((tm, LANE), lambda i: (i, 0)),
                   pl.BlockSpec((tm, LANE), lambda i: (i, 0))],
        out_shape=[jax.ShapeDtypeStruct((n, LANE), F32)] * 2,
        compiler_params=_cparams(("parallel",)),
        name="rope_tables",
    )(pos, invf)


def _gelu(x):
    return 0.5 * x * (1.0 + lax.erf(x * (1.0 / math.sqrt(2.0))))


def _sgu_kernel(u_ref, v_ref, g_ref, w_ref, b_ref, o_ref):
    tm = u_ref.shape[0]
    u = _gelu(u_ref[...].astype(F32))
    v = _gelu(v_ref[...].astype(F32))
    ms = jnp.mean(v * v, axis=-1, keepdims=True)
    v = v * lax.rsqrt(ms + EPS) * g_ref[...]
    lo = (lax.broadcasted_iota(jnp.int32, v.shape, 1) % LANE) < (A_W // A_G)
    v_lo = jnp.where(lo, v, 0.0).astype(BF16)
    v_hi = jnp.where(lo, 0.0, v).astype(BF16)
    r = lax.broadcasted_iota(jnp.int32, (A_T, 2 * A_T), 0)
    c = lax.broadcasted_iota(jnp.int32, (A_T, 2 * A_T), 1)
    causal = (c % A_T) <= r
    for p in range(A_G // 2):
        cols = slice(p * LANE, (p + 1) * LANE)
        w = jnp.where(causal, w_ref[p], 0.0).astype(BF16)
        for ch in range(tm // A_T):
            rows = slice(ch * A_T, (ch + 1) * A_T)
            rhs = jnp.concatenate([v_lo[rows, cols], v_hi[rows, cols]], axis=0)
            sv = jnp.dot(w, rhs, preferred_element_type=F32) + b_ref[:, cols]
            o_ref[rows, cols] = (u[rows, cols] * sv).astype(o_ref.dtype)


def _sgu(proj, g, wcat, bias, layer, tm=512):
    n = proj.shape[0]
    return pl.pallas_call(
        _sgu_kernel,
        grid=(n // tm,),
        in_specs=[pl.BlockSpec((tm, A_W), lambda i: (i, U_AU // 4)),
                  pl.BlockSpec((tm, A_W), lambda i: (i, U_AV // 4)),
                  _layer_spec((1, A_W), layer),
                  _layer_spec((A_G // 2, A_T, 2 * A_T), layer),
                  _layer_spec((A_T, A_W), layer)],
        out_specs=pl.BlockSpec((tm, A_W), lambda i: (i, 0)),
        out_shape=jax.ShapeDtypeStruct((n, A_W), BF16),
        compiler_params=_cparams(("parallel",)),
        name="sgu",
    )(proj, proj, g, wcat, bias)


def _rms(x, g):
    ms = jnp.mean(x * x, axis=-1, keepdims=True)
    return x * lax.rsqrt(ms + EPS) * g


def _mla_prep_kernel(p_ref, c_ref, s_ref, gq_ref, gkv_ref, wq_ref, wk_ref, wv_ref, qg_ref, kg_ref,
                     q_ref, k_ref, v_ref):
    p = p_ref[...].astype(F32)
    ql = p[:, :Q_LORA]
    kvl = p[:, Q_LORA:Q_LORA + KV_LORA]
    kr = p[:, Q_LORA + KV_LORA:]
    qn = _rms(ql, gq_ref[...]).astype(BF16)
    kvn = _rms(kvl, gkv_ref[...]).astype(BF16)
    q = jnp.dot(qn, wq_ref[...], preferred_element_type=F32)
    kn = jnp.dot(kvn, wk_ref[...], preferred_element_type=F32)
    v = jnp.dot(kvn, wv_ref[...], preferred_element_type=F32)
    vlane = lax.broadcasted_iota(jnp.int32, v.shape, 1) % LANE
    v_ref[...] = jnp.where(vlane == VD, 1.0, v).astype(v_ref.dtype)
    cs = c_ref[...]
    sn = s_ref[...]
    lane = lax.broadcasted_iota(jnp.int32, cs.shape, 1)
    upper = lane >= NOPE + ROPE // 2
    scale = QK ** -0.5 * math.log2(math.e)

    q_tabs = (cs * qg_ref[0:1, :], sn * qg_ref[1:2, :])
    k_tabs = (cs * kg_ref[0:1, :], sn * kg_ref[1:2, :])

    tiles = []
    for h in range(H):
        cols = slice(h * LANE, (h + 1) * LANE)
        tiles.append((q_ref, cols, q[:, cols], q_tabs, scale))
        tiles.append((k_ref, cols, kn[:, cols] + kr, k_tabs, 1.0))
    sums = [jnp.sum(x * x, axis=-1, keepdims=True) for _, _, x, _, _ in tiles]
    rs = [lax.rsqrt(ss * (1.0 / QK) + EPS) * sc for ss, (_, _, _, _, sc) in zip(sums, tiles)]
    rot = [x * tabs[0]
           + jnp.where(upper, pltpu.roll(x, ROPE // 2, 1), pltpu.roll(x, LANE - ROPE // 2, 1)) * tabs[1]
           for _, _, x, tabs, _ in tiles]
    for (ref, cols, _, _, _), y, r in zip(tiles, rot, rs):
        ref[:, cols] = (y * r).astype(ref.dtype)


def _mla_prep(proj, ctab, stab, gq, gkv, wq, wk, wv, qg, kg, layer, tm=512):
    n = proj.shape[0]
    wm = 6 * LANE
    full = lambda shp: _layer_spec(shp, layer)
    return pl.pallas_call(
        _mla_prep_kernel,
        grid=(n // tm,),
        in_specs=[pl.BlockSpec((tm, wm), lambda i: (i, 0)),
                  pl.BlockSpec((tm, LANE), lambda i: (i, 0)),
                  pl.BlockSpec((tm, LANE), lambda i: (i, 0)),
                  full((1, Q_LORA)), full((1, KV_LORA)),
                  full((Q_LORA, H * LANE)), full((KV_LORA, H * LANE)), full((KV_LORA, H * LANE)),
                  full((2, LANE)), full((2, LANE))],
        out_specs=[pl.BlockSpec((tm, H * LANE), lambda i: (i, 0))] * 3,
        out_shape=[jax.ShapeDtypeStruct((n, H * LANE), BF16)] * 3,
        compiler_params=_cparams(("parallel",)),
        name="mla_prep",
    )(proj, ctab, stab, gq, gkv, wq, wk, wv, qg, kg)


def _attn_kernel(q_ref, k_ref, v_ref, o_ref, *, t):
    qi = pl.program_id(2)
    heads = (slice(0, LANE), slice(LANE, 2 * LANE))
    qs = [q_ref[:, c] for c in heads]

    def step(start, w, carry, masked):
        ss = [lax.dot_general(q, k_ref[pl.ds(start, w), c], (((1,), (1,)), ((), ())),
                              preferred_element_type=F32) for c, q in zip(heads, qs)]
        if masked:
            row = lax.broadcasted_iota(jnp.int32, (t, w), 0)
            col = lax.broadcasted_iota(jnp.int32, (t, w), 1)
            ok = col <= row + (w - t)
            ss = [jnp.where(ok, s, -jnp.inf) for s in ss]
        ms = [jnp.maximum(m, jnp.max(s, axis=-1, keepdims=True)) for s, (m, _) in zip(ss, carry)]
        ps = [jnp.exp2(s - m_new).astype(BF16) for s, m_new in zip(ss, ms)]
        pv = [jnp.dot(p, v_ref[pl.ds(start, w), c], preferred_element_type=F32) for p, c in zip(ps, heads)]
        return tuple((m_new, jnp.exp2(m - m_new) * acc + x) for m_new, (m, acc), x in zip(ms, carry, pv))

    init = tuple((jnp.full((t, 1), -jnp.inf, F32), jnp.zeros((t, LANE), F32)) for _ in heads)
    carry = lax.fori_loop(
        0, qi // 2, lambda j, c: step(pl.multiple_of(j * (2 * t), 2 * t), 2 * t, c, False), init)
    (_, acc0), (_, acc1) = lax.cond(
        qi % 2 == 1,
        lambda c: step(pl.multiple_of((qi - 1) * t, 2 * t), 2 * t, c, True),
        lambda c: step(pl.multiple_of(qi * t, t), t, c, True),
        carry)
    out0 = acc0 / acc0[:, VD:VD + 1]
    out1 = acc1 / acc1[:, VD:VD + 1]
    lane = lax.broadcasted_iota(jnp.int32, (t, LANE), 1)
    o_ref[...] = jnp.where(lane < VD, out0, pltpu.roll(out1, VD, 1)).astype(o_ref.dtype)


def _attention(q, k, v, batch, seq, t=1024):
    n = q.shape[0]
    nq = seq // t
    k3 = k.reshape(batch, seq, H * LANE)
    v3 = v.reshape(batch, seq, H * LANE)
    return pl.pallas_call(
        functools.partial(_attn_kernel, t=t),
        grid=(batch, H // 2, nq),
        in_specs=[pl.BlockSpec((t, 2 * LANE), lambda b, hp, i: (b * nq + i, hp)),
                  pl.BlockSpec((None, seq, 2 * LANE), lambda b, hp, i: (b, 0, hp), pipeline_mode=pl.Buffered(1)),
                  pl.BlockSpec((None, seq, 2 * LANE), lambda b, hp, i: (b, 0, hp), pipeline_mode=pl.Buffered(1))],
        out_specs=pl.BlockSpec((t, 2 * VD), lambda b, hp, i: (b * nq + i, hp)),
        out_shape=jax.ShapeDtypeStruct((n, H * VD), BF16),
        compiler_params=_cparams(("parallel", "parallel", "arbitrary")),
        name="attention",
    )(q, k3, v3)


GW = 4 * DK


def _blockdiag(x, mask):
    return jnp.where(mask, jnp.concatenate([x] * (GW // CC), axis=0), jnp.zeros((), x.dtype))


def _split3(x):
    hi = x.astype(BF16)
    r = x - hi.astype(F32)
    mid = r.astype(BF16)
    lo = (r - mid.astype(F32)).astype(BF16)
    return hi, mid, lo


def _dot_exact_rhs(sel, x):
    n = x.shape[1]
    r = jnp.dot(sel, jnp.concatenate(_split3(x), axis=1), preferred_element_type=F32)
    return r[:, :n] + r[:, n:2 * n] + r[:, 2 * n:]


def _bd_dot(a, b, mask):
    return jnp.dot(a.astype(BF16), _blockdiag(b.astype(BF16), mask), preferred_element_type=F32)


def _segsum(x, e):
    return jnp.dot(x.astype(BF16), e, preferred_element_type=F32)


def _deltanet_kernel(qkv_ref, z_ref, b_ref, a_ref, cw_ref, alog_ref, dtb_ref, og_ref, e_ref,
                     o_ref, tail_ref, state_ref, *, tm):
    t_idx = pl.program_id(1)
    nch = tm // CC
    ng = (H * DK) // GW
    width = 3 * H * DK

    @pl.when(t_idx == 0)
    def _():
        tail_ref[...] = jnp.zeros(tail_ref.shape, F32)
        state_ref[...] = jnp.zeros(state_ref.shape, F32)

    xb = qkv_ref[...]
    x = xb.astype(F32)
    sr = lax.broadcasted_iota(jnp.int32, ((CONV - 1) * tm, tm), 0)
    sc = lax.broadcasted_iota(jnp.int32, ((CONV - 1) * tm, tm), 1)
    sel = jnp.where(sc == (sr % tm) - (CONV - 1 - sr // tm), 1.0, 0.0).astype(BF16)
    shifted = jnp.dot(sel, xb, preferred_element_type=F32)
    xc = cw_ref[CONV - 1:CONV, :] * x
    for j in range(CONV - 1):
        xc = xc + cw_ref[j:j + 1, :] * shifted[j * tm:(j + 1) * tm]
    head = jnp.concatenate([tail_ref[...], x[0:8]], axis=0)
    fix = cw_ref[CONV - 1:CONV, :] * x[0:8]
    for j in range(CONV - 1):
        fix = fix + cw_ref[j:j + 1, :] * head[8 - (CONV - 1) + j: 16 - (CONV - 1) + j]
    xc = jnp.concatenate([fix, xc[8:]], axis=0)
    tail_ref[...] = x[tm - 8:tm]
    xc = xc * jax.nn.sigmoid(xc)
    e = e_ref[...]
    hw = H * DK
    q = xc[:, :hw]
    k = xc[:, hw:2 * hw]
    v = xc[:, 2 * hw:]
    q = q * lax.rsqrt(_segsum(q * q, e) + EPS) * (DK ** -0.5)
    k = k * lax.rsqrt(_segsum(k * k, e) + EPS)
    beta = jax.nn.sigmoid(b_ref[...].astype(F32))
    sp_in = a_ref[...].astype(F32) + dtb_ref[...]
    softplus = jnp.maximum(sp_in, 0.0) + jnp.log1p(jnp.exp(-jnp.abs(sp_in)))
    g = -jnp.exp(alog_ref[...]) * softplus

    ri = lax.broadcasted_iota(jnp.int32, (tm, tm), 0)
    ci = lax.broadcasted_iota(jnp.int32, (tm, tm), 1)
    cum = jnp.where((ri // CC == ci // CC) & (ci <= ri), 1.0, 0.0).astype(BF16)
    gc = _dot_exact_rhs(cum, g)
    egc = jnp.exp(gc)
    kb = k * beta
    vb = v * beta
    kbg = kb * egc
    qg = q * egc

    r64 = lax.broadcasted_iota(jnp.int32, (CC, GW), 0)
    l64 = lax.broadcasted_iota(jnp.int32, (CC, GW), 1) % CC
    tril = l64 <= r64
    strict = l64 < r64
    eye_t = jnp.where(l64 == r64, 1.0, 0.0).astype(F32)
    rb = lax.broadcasted_iota(jnp.int32, (GW, GW), 0) // CC
    lb = lax.broadcasted_iota(jnp.int32, (GW, GW), 1) // CC
    bd = rb == lb
    ones = jnp.ones((CC, CC), BF16)

    tiles = [(slice(ch * CC, (ch + 1) * CC), slice(gi * GW, (gi + 1) * GW), ch, gi)
             for ch in range(nch) for gi in range(ng)]
    low_l, intra_l = [], []
    for rows, cols, _, _ in tiles:
        gc_c = gc[rows, cols]
        bdk = _blockdiag(k[rows, cols].astype(BF16), bd)
        lhs = jnp.concatenate([kb[rows, cols], q[rows, cols]], axis=0).astype(BF16)
        a2 = lax.dot_general(lhs, bdk, (((1,), (1,)), ((), ())), preferred_element_type=F32)
        gct = _dot_exact_rhs(ones, jnp.where(l64 == r64, gc_c, 0.0))
        decay = jnp.exp(jnp.where(tril, gc_c - gct, -jnp.inf))
        low_l.append(jnp.where(strict, a2[:CC] * decay, 0.0))
        intra_l.append(jnp.where(tril, a2[CC:] * decay, 0.0).astype(BF16))
    n_tiles = len(tiles)
    diag_l = [jnp.where(r64 // SB == l64 // SB, low, 0.0) for low in low_l]
    p_l = [eye_t - d for d in diag_l]
    m_l = [_bd_dot(d, d, bd) for d in diag_l]
    n_sq = int(math.log2(SB)) - 1
    for lvl in range(n_sq):
        for i in range(n_tiles):
            if lvl < n_sq - 1:
                r = _bd_dot(jnp.concatenate([p_l[i], m_l[i]], axis=0), m_l[i], bd)
                p_l[i] = p_l[i] + r[:CC]
                m_l[i] = r[CC:]
            else:
                p_l[i] = p_l[i] + _bd_dot(p_l[i], m_l[i], bd)
    size = SB
    while size < CC:
        off = (r64 // (2 * size) == l64 // (2 * size)) & (r64 // size != l64 // size)
        x_l = [_bd_dot(p, jnp.where(off, low, 0.0), bd) for p, low in zip(p_l, low_l)]
        p_l = [p - _bd_dot(x, p, bd) for p, x in zip(p_l, x_l)]
        size *= 2
    uw_l = []
    for (rows, cols, _, _), p_acc in zip(tiles, p_l):
        rhs = jnp.concatenate([_blockdiag(vb[rows, cols].astype(BF16), bd),
                               _blockdiag(kbg[rows, cols].astype(BF16), bd)], axis=1)
        uw_l.append(jnp.dot(p_acc.astype(BF16), rhs, preferred_element_type=F32))

    states = [state_ref[gi] for gi in range(ng)]
    o_l = []
    for (rows, cols, ch, gi), uw, intra in zip(tiles, uw_l, intra_l):
        st = states[gi]
        lhs2 = jnp.concatenate([uw[:, GW:], qg[rows, cols]], axis=0).astype(BF16)
        ws = jnp.dot(lhs2, st.astype(BF16), preferred_element_type=F32)
        v_new = (uw[:, :GW] - ws[:CC]).astype(BF16)
        o = ws[CC:] + jnp.dot(intra, _blockdiag(v_new, bd), preferred_element_type=F32)
        g_last = gc[ch * CC + CC - 1: ch * CC + CC, cols]
        kd = (k[rows, cols] * jnp.exp(g_last - gc[rows, cols])).astype(BF16)
        upd = lax.dot_general(kd, v_new, (((0,), (0,)), ((), ())), preferred_element_type=F32)
        states[gi] = st * jnp.exp(g_last) + jnp.where(bd, upd, 0.0)
        o_l.append(o)
    for gi in range(ng):
        state_ref[gi] = states[gi]

    o = jnp.concatenate([jnp.concatenate(o_l[ch * ng:(ch + 1) * ng], axis=1) for ch in range(nch)], axis=0)
    o = o * lax.rsqrt(_segsum(o * o, e) * (1.0 / DV) + EPS) * og_ref[...]
    z = z_ref[...].astype(F32)
    o_ref[...] = (o * (z * jax.nn.sigmoid(z))).astype(o_ref.dtype)


def _deltanet(proj, conv_w, alog, dtb, og, layer, batch, seq, tm=256):
    n = proj.shape[0]
    nt = seq // tm
    hw = H * DK
    e = jnp.asarray(np.kron(np.eye(H), np.ones((DK, DK))), BF16)
    full = lambda shp: _layer_spec(shp, layer)
    blk = lambda unit, w: pl.BlockSpec((tm, w), lambda b, i: (b * nt + i, unit * LANE // w))
    return pl.pallas_call(
        functools.partial(_deltanet_kernel, tm=tm),
        grid=(batch, nt),
        in_specs=[blk(U_QKV, 3 * hw), blk(U_CZ, hw), blk(U_CB, hw), blk(U_CA, hw),
                  full((CONV, 3 * hw)), full((1, hw)), full((1, hw)), full((1, hw)),
                  pl.BlockSpec((hw, hw), lambda b, i: (0, 0))],
        out_specs=pl.BlockSpec((tm, hw), lambda b, i: (b * nt + i, 0)),
        out_shape=jax.ShapeDtypeStruct((n, hw), BF16),
        scratch_shapes=[pltpu.VMEM((8, 3 * hw), F32),
                        pltpu.VMEM((hw // GW, GW, GW), F32)],
        compiler_params=_cparams(("parallel", "arbitrary")),
        name="deltanet",
    )(proj, proj, proj, proj, conv_w, alog, dtb, og, e)


def _mix_ffn_kernel(x_ref, ya_ref, yb_ref, yc_ref, g0_ref, g1_ref, g2_ref, wb_ref, wo_ref, gn_ref,
                    w1_ref, w2_ref, o_ref, a_ref, *, tf):
    merged = None
    for i, (y_ref, g_ref) in enumerate(((ya_ref, g0_ref), (yb_ref, g1_ref), (yc_ref, g2_ref))):
        y = jnp.dot(y_ref[...], wb_ref[i], preferred_element_type=F32)
        term = jax.nn.sigmoid(g_ref[...].astype(F32)) * y
        merged = term if merged is None else merged + term
    x1 = x_ref[...] + jnp.dot(merged.astype(BF16), wo_ref[...], preferred_element_type=F32)
    h = _rms(x1, gn_ref[...]).astype(BF16)
    for f in range(DFF // tf):
        cols = slice(f * tf, (f + 1) * tf)
        a = jnp.dot(h, w1_ref[:, cols], preferred_element_type=F32)
        a_ref[:, cols] = jnp.square(jnp.maximum(a, 0.0)).astype(BF16)
    o_ref[...] = x1 + jnp.dot(a_ref[...], w2_ref[...], preferred_element_type=F32)


def _mix_ffn(x, ya, yb, yc, proj, wb, wo, gn, w1, w2, layer, tm=512, tf=1024):
    n = x.shape[0]
    tok = lambda w: pl.BlockSpec((tm, w), lambda r: (r, 0))
    gate = lambda i: pl.BlockSpec((tm, D), lambda r, i=i: (r, U_GATE // 8 + i))
    res = lambda shp: _layer_spec(shp, layer, single_buffer=True)
    return pl.pallas_call(
        functools.partial(_mix_ffn_kernel, tf=tf),
        grid=(n // tm,),
        in_specs=[tok(D), tok(BW), tok(BW), tok(BW), gate(0), gate(1), gate(2),
                  res((NB, BW, D)), res((D, D)), res((1, D)), res((D, DFF)), res((DFF, D))],
        out_specs=tok(D),
        out_shape=jax.ShapeDtypeStruct((n, D), F32),
        scratch_shapes=[pltpu.VMEM((tm, DFF), BF16)],
        compiler_params=_cparams(("parallel",)),
        name="mix_ffn",
    )(x, ya, yb, yc, proj, proj, proj, wb, wo, gn, w1, w2)


def _pad_heads(w, per_head, first, used):
    depth, kdim, _ = w.shape
    w = w.reshape(depth, kdim, H, per_head)[..., first:first + used]
    w = jnp.pad(w, ((0, 0), (0, 0), (0, 0), (0, LANE - used)))
    return w.reshape(depth, kdim, H * LANE).astype(BF16)


def kernel(x, positions, norm1_g, w_in, sgu_norm_g, w_spatial, b_spatial, q_lat_norm_g, w_q_up,
           kv_lat_norm_g, w_kv_up, q_norm_g, k_norm_g, conv_w, a_log, dt_bias, o_norm_g,
           w_branch, w_out, norm2_g, w_ff1, w_ff2):
    batch, seq, d = x.shape
    depth = w_in.shape[0]
    n = batch * seq
    xt = x.reshape(n, d)
    ctab, stab = _rope_tables(positions.reshape(n, 1))
    rows = lambda a: a.reshape(depth, 1, -1).astype(F32)
    half = ROPE // 2
    w_perm = _permute_w_in(w_in)
    wcat = (w_spatial.reshape(depth, A_G // 2, 2, A_T, A_T).transpose(0, 1, 3, 2, 4)
            .reshape(depth, A_G // 2, A_T, 2 * A_T).astype(F32))
    sgu_bias = jnp.repeat(jnp.swapaxes(b_spatial, 1, 2), A_W // A_G, axis=2).astype(F32)
    wq = _pad_heads(w_q_up, QK, 0, QK)
    wk = _pad_heads(w_kv_up, NOPE + VD, 0, NOPE)
    wv = _pad_heads(w_kv_up, NOPE + VD, NOPE, VD)
    pad_g = lambda g: jnp.pad(
        jnp.stack([g, jnp.concatenate([g[:, :NOPE], g[:, NOPE + half:], g[:, NOPE:NOPE + half]], axis=1)],
                  axis=1).astype(F32), ((0, 0), (0, 0), (0, LANE - QK)))
    qg, kg = pad_g(q_norm_g), pad_g(k_norm_g)
    rep = lambda a: jnp.repeat(a.astype(F32), DK, axis=1).reshape(depth, 1, H * DK)
    alog, dtb = rep(a_log), rep(dt_bias)
    og = jnp.tile(o_norm_g.astype(F32), (1, H)).reshape(depth, 1, H * DV)
    conv_f = conv_w.astype(F32)
    wb, wo, w1, w2 = (w.astype(BF16) for w in (w_branch, w_out, w_ff1, w_ff2))
    g1, g2, gs, gq, gkv = (rows(g) for g in (norm1_g, norm2_g, sgu_norm_g, q_lat_norm_g, kv_lat_norm_g))

    for l in range(depth):
        proj = _inproj(xt, g1, w_perm, l)
        y_a = _sgu(proj, gs, wcat, sgu_bias, l)
        q, k, v = _mla_prep(proj, ctab, stab, gq, gkv, wq, wk, wv, qg, kg, l)
        y_b = _attention(q, k, v, batch, seq)
        y_c = _deltanet(proj, conv_f, alog, dtb, og, l, batch, seq)
        xt = _mix_ffn(xt, y_a, y_b, y_c, proj, wb, wo, g2, w1, w2, l)
    return xt.reshape(batch, seq, d)
```

```python
import functools
import math

import numpy as np
import jax
import jax.numpy as jnp
from jax import lax
from jax.experimental import pallas as pl
from jax.experimental.pallas import tpu as pltpu

F32 = jnp.float32
BF16 = jnp.bfloat16

LANE = 128
VMEM_LIMIT = 56 * 1024 * 1024

D = 1024
A_W = 512
A_G = 8
A_T = 128
H = 8
NOPE = 64
ROPE = 32
VD = 64
QK = NOPE + ROPE
Q_LORA = 384
KV_LORA = 256
ROPE_BASE = 10000.0
DK = 64
DV = 64
CONV = 4
CC = 64
SB = 16
NB = 3
BW = 512
DFF = 4 * D
EPS = 1e-6

U_MLA, U_BA, U_CZ, U_QKV, U_AU, U_AV, U_GATE = 0, 6, 8, 12, 24, 28, 32
DP = 56 * LANE

O_AU, O_AV, O_QLAT, O_KVLAT, O_KROPE = 0, 512, 1024, 1408, 1664
O_CQKV, O_CZ, O_CB, O_CA, O_GATES = 1696, 3232, 3744, 3752, 3760


def _permute_w_in(w):
    w = w.astype(BF16)
    sl = lambda off, n: w[..., off:off + n]
    zeros = lambda n: jnp.zeros(w.shape[:-1] + (n,), w.dtype)
    pieces = [
        sl(O_QLAT, Q_LORA), sl(O_KVLAT, KV_LORA),
        zeros(NOPE), sl(O_KROPE, ROPE), zeros(LANE - QK),
        sl(O_CB, H), sl(O_CA, H), zeros((U_CZ - U_BA) * LANE - 2 * H),
        sl(O_CZ, 512), sl(O_CQKV, 1536), sl(O_AU, A_W), sl(O_AV, A_W),
        sl(O_GATES, NB * D),
    ]
    out = jnp.concatenate(pieces, axis=-1).astype(BF16)
    assert out.shape[-1] == DP
    return out


def _cparams(sem):
    return pltpu.CompilerParams(dimension_semantics=sem, vmem_limit_bytes=VMEM_LIMIT)


def _layer_spec(shape, layer, single_buffer=False):
    index_map = lambda *_: (layer,) + (0,) * len(shape)
    if single_buffer:
        return pl.BlockSpec((None,) + shape, index_map, pipeline_mode=pl.Buffered(1))
    return pl.BlockSpec((None,) + shape, index_map)


def _inproj_kernel(x_ref, g_ref, w_ref, o_ref, *, tn):
    x = x_ref[...]
    ms = jnp.mean(x * x, axis=-1, keepdims=True)
    h = (x * lax.rsqrt(ms + EPS) * g_ref[...]).astype(BF16)
    for j in range(w_ref.shape[1] // tn):
        cols = slice(j * tn, (j + 1) * tn)
        o_ref[:, cols] = jnp.dot(h, w_ref[:, cols], preferred_element_type=F32).astype(o_ref.dtype)


def _inproj(x, g, w, layer, tm=512, tn=1024):
    n = x.shape[0]
    dp = w.shape[-1]
    return pl.pallas_call(
        functools.partial(_inproj_kernel, tn=tn),
        grid=(n // tm,),
        in_specs=[pl.BlockSpec((tm, D), lambda i: (i, 0)),
                  _layer_spec((1, D), layer),
                  _layer_spec((D, dp), layer, single_buffer=True)],
        out_specs=pl.BlockSpec((tm, dp), lambda i: (i, 0)),
        out_shape=jax.ShapeDtypeStruct((n, dp), BF16),
        compiler_params=_cparams(("parallel",)),
        name="inproj",
    )(x, g, w)


def _rope_kernel(pos_ref, invf_ref, c_ref, s_ref):
    pos = pos_ref[...].astype(F32)
    ang = pos * invf_ref[...]
    lane = lax.broadcasted_iota(jnp.int32, ang.shape, 1)
    c = jnp.cos(ang)
    s = jnp.sin(ang)
    half = ROPE // 2
    c_ref[...] = jnp.where((lane >= NOPE) & (lane < QK), c, 1.0)
    s_ref[...] = jnp.where(lane < NOPE, 0.0,
                           jnp.where(lane < NOPE + half, -s, jnp.where(lane < QK, s, 0.0)))


def _rope_tables(pos, tm=1024):
    n = pos.shape[0]
    half = ROPE // 2
    lane = np.arange(LANE)
    invf = np.where((lane >= NOPE) & (lane < QK),
                    1.0 / (ROPE_BASE ** (((lane - NOPE) % half).astype(np.float32) / half)), 0.0)
    invf = jnp.asarray(invf, F32).reshape(1, LANE)
    return pl.pallas_call(
        _rope_kernel,
        grid=(n // tm,),
        in_specs=[pl.BlockSpec((tm, 1), lambda i: (i, 0)),
                  pl.BlockSpec((1, LANE), lambda i: (0, 0))],
        out_specs=[pl.BlockSpec((tm, LANE), lambda i: (i, 0)),
                   pl.BlockSpec((tm, LANE), lambda i: (i, 0))],
        out_shape=[jax.ShapeDtypeStruct((n, LANE), F32)] * 2,
        compiler_params=_cparams(("parallel",)),
        name="rope_tables",
    )(pos, invf)


def _gelu(x):
    return 0.5 * x * (1.0 + lax.erf(x * (1.0 / math.sqrt(2.0))))


def _sgu_kernel(u_ref, v_ref, g_ref, w_ref, b_ref, o_ref):
    tm = u_ref.shape[0]
    u = _gelu(u_ref[...].astype(F32))
    v = _gelu(v_ref[...].astype(F32))
    ms = jnp.mean(v * v, axis=-1, keepdims=True)
    v = v * lax.rsqrt(ms + EPS) * g_ref[...]
    lo = (lax.broadcasted_iota(jnp.int32, v.shape, 1) % LANE) < (A_W // A_G)
    v_lo = jnp.where(lo, v, 0.0).astype(BF16)
    v_hi = jnp.where(lo, 0.0, v).astype(BF16)
    r = lax.broadcasted_iota(jnp.int32, (A_T, 2 * A_T), 0)
    c = lax.broadcasted_iota(jnp.int32, (A_T, 2 * A_T), 1)
    causal = (c % A_T) <= r
    for p in range(A_G // 2):
        cols = slice(p * LANE, (p + 1) * LANE)
        w = jnp.where(causal, w_ref[p], 0.0).astype(BF16)
        for ch in range(tm // A_T):
            rows = slice(ch * A_T, (ch + 1) * A_T)
            rhs = jnp.concatenate([v_lo[rows, cols], v_hi[rows, cols]], axis=0)
            sv = jnp.dot(w, rhs, preferred_element_type=F32) + b_ref[:, cols]
            o_ref[rows, cols] = (u[rows, cols] * sv).astype(o_ref.dtype)


def _sgu(proj, g, wcat, bias, layer, tm=2048):
    n = proj.shape[0]
    return pl.pallas_call(
        _sgu_kernel,
        grid=(n // tm,),
        in_specs=[pl.BlockSpec((tm, A_W), lambda i: (i, U_AU // 4)),
                  pl.BlockSpec((tm, A_W), lambda i: (i, U_AV // 4)),
                  _layer_spec((1, A_W), layer),
                  _layer_spec((A_G // 2, A_T, 2 * A_T), layer),
                  _layer_spec((A_T, A_W), layer)],
        out_specs=pl.BlockSpec((tm, A_W), lambda i: (i, 0)),
        out_shape=jax.ShapeDtypeStruct((n, A_W), BF16),
        compiler_params=_cparams(("parallel",)),
        name="sgu",
    )(proj, proj, g, wcat, bias)


def _rms(x, g):
    ms = jnp.mean(x * x, axis=-1, keepdims=True)
    return x * lax.rsqrt(ms + EPS) * g


def _mla_prep_kernel(p_ref, c_ref, s_ref, gq_ref, gkv_ref, wq_ref, wk_ref, wv_ref, qg_ref, kg_ref,
                     q_ref, k_ref, v_ref):
    p = p_ref[...].astype(F32)
    ql = p[:, :Q_LORA]
    kvl = p[:, Q_LORA:Q_LORA + KV_LORA]
    kr = p[:, Q_LORA + KV_LORA:]
    qn = _rms(ql, gq_ref[...]).astype(BF16)
    kvn = _rms(kvl, gkv_ref[...]).astype(BF16)
    q = jnp.dot(qn, wq_ref[...], preferred_element_type=F32)
    kn = jnp.dot(kvn, wk_ref[...], preferred_element_type=F32)
    v = jnp.dot(kvn, wv_ref[...], preferred_element_type=F32)
    vlane = lax.broadcasted_iota(jnp.int32, v.shape, 1) % LANE
    v_ref[...] = jnp.where(vlane == VD, 1.0, v).astype(v_ref.dtype)
    cs = c_ref[...]
    sn = s_ref[...]
    lane = lax.broadcasted_iota(jnp.int32, cs.shape, 1)
    upper = lane >= NOPE + ROPE // 2
    scale = QK ** -0.5 * math.log2(math.e)

    q_tabs = (cs * qg_ref[0:1, :], sn * qg_ref[1:2, :])
    k_tabs = (cs * kg_ref[0:1, :], sn * kg_ref[1:2, :])

    tiles = []
    for h in range(H):
        cols = slice(h * LANE, (h + 1) * LANE)
        tiles.append((q_ref, cols, q[:, cols], q_tabs, scale))
        tiles.append((k_ref, cols, kn[:, cols] + kr, k_tabs, 1.0))
    sums = [jnp.sum(x * x, axis=-1, keepdims=True) for _, _, x, _, _ in tiles]
    rs = [lax.rsqrt(ss * (1.0 / QK) + EPS) * sc for ss, (_, _, _, _, sc) in zip(sums, tiles)]
    rot = [x * tabs[0]
           + jnp.where(upper, pltpu.roll(x, ROPE // 2, 1), pltpu.roll(x, LANE - ROPE // 2, 1)) * tabs[1]
           for _, _, x, tabs, _ in tiles]
    for (ref, cols, _, _, _), y, r in zip(tiles, rot, rs):
        ref[:, cols] = (y * r).astype(ref.dtype)


def _mla_prep(proj, ctab, stab, gq, gkv, wq, wk, wv, qg, kg, layer, tm=1024):
    n = proj.shape[0]
    wm = 6 * LANE
    full = lambda shp: _layer_spec(shp, layer)
    return pl.pallas_call(
        _mla_prep_kernel,
        grid=(n // tm,),
        in_specs=[pl.BlockSpec((tm, wm), lambda i: (i, 0)),
                  pl.BlockSpec((tm, LANE), lambda i: (i, 0)),
                  pl.BlockSpec((tm, LANE), lambda i: (i, 0)),
                  full((1, Q_LORA)), full((1, KV_LORA)),
                  full((Q_LORA, H * LANE)), full((KV_LORA, H * LANE)), full((KV_LORA, H * LANE)),
                  full((2, LANE)), full((2, LANE))],
        out_specs=[pl.BlockSpec((tm, H * LANE), lambda i: (i, 0))] * 3,
        out_shape=[jax.ShapeDtypeStruct((n, H * LANE), BF16)] * 3,
        compiler_params=_cparams(("parallel",)),
        name="mla_prep",
    )(proj, ctab, stab, gq, gkv, wq, wk, wv, qg, kg)


def _attn_kernel(q_ref, k_ref, v_ref, o_ref, *, t):
    qi = pl.program_id(2)
    heads = (slice(0, LANE), slice(LANE, 2 * LANE))
    qs = [q_ref[:, c] for c in heads]

    def step(start, w, carry, masked):
        ss = [lax.dot_general(q, k_ref[pl.ds(start, w), c], (((1,), (1,)), ((), ())),
                              preferred_element_type=F32) for c, q in zip(heads, qs)]
        if masked:
            row = lax.broadcasted_iota(jnp.int32, (t, w), 0)
            col = lax.broadcasted_iota(jnp.int32, (t, w), 1)
            ok = col <= row + (w - t)
            ss = [jnp.where(ok, s, -jnp.inf) for s in ss]
        ms = [jnp.maximum(m, jnp.max(s, axis=-1, keepdims=True)) for s, (m, _) in zip(ss, carry)]
        ps = [jnp.exp2(s - m_new).astype(BF16) for s, m_new in zip(ss, ms)]
        pv = [jnp.dot(p, v_ref[pl.ds(start, w), c], preferred_element_type=F32) for p, c in zip(ps, heads)]
        return tuple((m_new, jnp.exp2(m - m_new) * acc + x) for m_new, (m, acc), x in zip(ms, carry, pv))

    init = tuple((jnp.full((t, 1), -jnp.inf, F32), jnp.zeros((t, LANE), F32)) for _ in heads)
    carry = lax.fori_loop(
        0, qi // 2, lambda j, c: step(pl.multiple_of(j * (2 * t), 2 * t), 2 * t, c, False), init)
    (_, acc0), (_, acc1) = lax.cond(
        qi % 2 == 1,
        lambda c: step(pl.multiple_of((qi - 1) * t, 2 * t), 2 * t, c, True),
        lambda c: step(pl.multiple_of(qi * t, t), t, c, True),
        carry)
    out0 = acc0 / acc0[:, VD:VD + 1]
    out1 = acc1 / acc1[:, VD:VD + 1]
    lane = lax.broadcasted_iota(jnp.int32, (t, LANE), 1)
    o_ref[...] = jnp.where(lane < VD, out0, pltpu.roll(out1, VD, 1)).astype(o_ref.dtype)


def _attention(q, k, v, batch, seq, t=1024):
    n = q.shape[0]
    nq = seq // t
    k3 = k.reshape(batch, seq, H * LANE)
    v3 = v.reshape(batch, seq, H * LANE)
    return pl.pallas_call(
        functools.partial(_attn_kernel, t=t),
        grid=(batch, H // 2, nq),
        in_specs=[pl.BlockSpec((t, 2 * LANE), lambda b, hp, i: (b * nq + i, hp)),
                  pl.BlockSpec((None, seq, 2 * LANE), lambda b, hp, i: (b, 0, hp), pipeline_mode=pl.Buffered(1)),
                  pl.BlockSpec((None, seq, 2 * LANE), lambda b, hp, i: (b, 0, hp), pipeline_mode=pl.Buffered(1))],
        out_specs=pl.BlockSpec((t, 2 * VD), lambda b, hp, i: (b * nq + i, hp)),
        out_shape=jax.ShapeDtypeStruct((n, H * VD), BF16),
        compiler_params=_cparams(("parallel", "parallel", "arbitrary")),
        name="attention",
    )(q, k3, v3)


GW = 4 * DK


def _blockdiag(x, mask):
    return jnp.where(mask, jnp.concatenate([x] * (GW // CC), axis=0), jnp.zeros((), x.dtype))


def _split3(x):
    hi = x.astype(BF16)
    r = x - hi.astype(F32)
    mid = r.astype(BF16)
    lo = (r - mid.astype(F32)).astype(BF16)
    return hi, mid, lo


def _dot_exact_rhs(sel, x):
    n = x.shape[1]
    r = jnp.dot(sel, jnp.concatenate(_split3(x), axis=1), preferred_element_type=F32)
    return r[:, :n] + r[:, n:2 * n] + r[:, 2 * n:]


def _bd_dot(a, b, mask):
    return jnp.dot(a.astype(BF16), _blockdiag(b.astype(BF16), mask), preferred_element_type=F32)


def _segsum(x, e):
    return jnp.dot(x.astype(BF16), e, preferred_element_type=F32)


def _deltanet_kernel(qkv_ref, z_ref, ba_ref, cw_ref, alog_ref, dtb_ref, og_ref, e_ref, ex_ref,
                     o_ref, tail_ref, state_ref, *, tm):
    t_idx = pl.program_id(1)
    nch = tm // CC
    ng = (H * DK) // GW
    width = 3 * H * DK

    @pl.when(t_idx == 0)
    def _():
        tail_ref[...] = jnp.zeros(tail_ref.shape, F32)
        state_ref[...] = jnp.zeros(state_ref.shape, F32)

    xb = qkv_ref[...]
    x = xb.astype(F32)
    sr = lax.broadcasted_iota(jnp.int32, ((CONV - 1) * tm, tm), 0)
    sc = lax.broadcasted_iota(jnp.int32, ((CONV - 1) * tm, tm), 1)
    sel = jnp.where(sc == (sr % tm) - (CONV - 1 - sr // tm), 1.0, 0.0).astype(BF16)
    shifted = jnp.dot(sel, xb, preferred_element_type=F32)
    xc = cw_ref[CONV - 1:CONV, :] * x
    for j in range(CONV - 1):
        xc = xc + cw_ref[j:j + 1, :] * shifted[j * tm:(j + 1) * tm]
    head = jnp.concatenate([tail_ref[...], x[0:8]], axis=0)
    fix = cw_ref[CONV - 1:CONV, :] * x[0:8]
    for j in range(CONV - 1):
        fix = fix + cw_ref[j:j + 1, :] * head[8 - (CONV - 1) + j: 16 - (CONV - 1) + j]
    xc = jnp.concatenate([fix, xc[8:]], axis=0)
    tail_ref[...] = x[tm - 8:tm]
    xc = xc * jax.nn.sigmoid(xc)
    e = e_ref[...]
    hw = H * DK
    q = xc[:, :hw]
    k = xc[:, hw:2 * hw]
    v = xc[:, 2 * hw:]
    q = q * lax.rsqrt(_segsum(q * q, e) + EPS) * (DK ** -0.5)
    k = k * lax.rsqrt(_segsum(k * k, e) + EPS)
    ba = ba_ref[...].astype(F32)
    sp_in = ba + dtb_ref[...]
    softplus = jnp.maximum(sp_in, 0.0) + jnp.log1p(jnp.exp(-jnp.abs(sp_in)))
    is_b = lax.broadcasted_iota(jnp.int32, ba.shape, 1) < H
    gates = jnp.where(is_b, jax.nn.sigmoid(ba), -jnp.exp(alog_ref[...]) * softplus)
    pieces = jnp.dot(jnp.concatenate(_split3(gates), axis=0), ex_ref[...], preferred_element_type=F32)
    spread = pieces[:tm] + pieces[tm:2 * tm] + pieces[2 * tm:]
    beta = spread[:, :hw]
    g = spread[:, hw:]

    ri = lax.broadcasted_iota(jnp.int32, (tm, tm), 0)
    ci = lax.broadcasted_iota(jnp.int32, (tm, tm), 1)
    cum = jnp.where((ri // CC == ci // CC) & (ci <= ri), 1.0, 0.0).astype(BF16)
    gc = _dot_exact_rhs(cum, g)
    egc = jnp.exp(gc)
    kb = k * beta
    vb = v * beta
    kbg = kb * egc
    qg = q * egc

    r64 = lax.broadcasted_iota(jnp.int32, (CC, GW), 0)
    l64 = lax.broadcasted_iota(jnp.int32, (CC, GW), 1) % CC
    tril = l64 <= r64
    strict = l64 < r64
    eye_t = jnp.where(l64 == r64, 1.0, 0.0).astype(F32)
    rb = lax.broadcasted_iota(jnp.int32, (GW, GW), 0) // CC
    lb = lax.broadcasted_iota(jnp.int32, (GW, GW), 1) // CC
    bd = rb == lb
    ones = jnp.ones((CC, CC), BF16)

    tiles = [(slice(ch * CC, (ch + 1) * CC), slice(gi * GW, (gi + 1) * GW), ch, gi)
             for ch in range(nch) for gi in range(ng)]
    low_l, intra_l = [], []
    for rows, cols, _, _ in tiles:
        gc_c = gc[rows, cols]
        bdk = _blockdiag(k[rows, cols].astype(BF16), bd)
        lhs = jnp.concatenate([kb[rows, cols], q[rows, cols]], axis=0).astype(BF16)
        a2 = lax.dot_general(lhs, bdk, (((1,), (1,)), ((), ())), preferred_element_type=F32)
        gct = _dot_exact_rhs(ones, jnp.where(l64 == r64, gc_c, 0.0))
        decay = jnp.exp(jnp.where(tril, gc_c - gct, -jnp.inf))
        low_l.append(jnp.where(strict, a2[:CC] * decay, 0.0))
        intra_l.append(jnp.where(tril, a2[CC:] * decay, 0.0).astype(BF16))
    n_tiles = len(tiles)
    diag_l = [jnp.where(r64 // SB == l64 // SB, low, 0.0) for low in low_l]
    p_l = [eye_t - d for d in diag_l]
    m_l = [_bd_dot(d, d, bd) for d in diag_l]
    n_sq = int(math.log2(SB)) - 1
    for lvl in range(n_sq):
        for i in range(n_tiles):
            if lvl < n_sq - 1:
                r = _bd_dot(jnp.concatenate([p_l[i], m_l[i]], axis=0), m_l[i], bd)
                p_l[i] = p_l[i] + r[:CC]
                m_l[i] = r[CC:]
            else:
                p_l[i] = p_l[i] + _bd_dot(p_l[i], m_l[i], bd)
    size = SB
    while size < CC:
        off = (r64 // (2 * size) == l64 // (2 * size)) & (r64 // size != l64 // size)
        x_l = [_bd_dot(p, jnp.where(off, low, 0.0), bd) for p, low in zip(p_l, low_l)]
        p_l = [p - _bd_dot(x, p, bd) for p, x in zip(p_l, x_l)]
        size *= 2
    uw_l = []
    for (rows, cols, _, _), p_acc in zip(tiles, p_l):
        rhs = jnp.concatenate([_blockdiag(vb[rows, cols].astype(BF16), bd),
                               _blockdiag(kbg[rows, cols].astype(BF16), bd)], axis=1)
        uw_l.append(jnp.dot(p_acc.astype(BF16), rhs, preferred_element_type=F32))

    states = [state_ref[gi] for gi in range(ng)]
    o_l = []
    for (rows, cols, ch, gi), uw, intra in zip(tiles, uw_l, intra_l):
        st = states[gi]
        lhs2 = jnp.concatenate([uw[:, GW:], qg[rows, cols]], axis=0).astype(BF16)
        ws = jnp.dot(lhs2, st.astype(BF16), preferred_element_type=F32)
        v_new = (uw[:, :GW] - ws[:CC]).astype(BF16)
        o = ws[CC:] + jnp.dot(intra, _blockdiag(v_new, bd), preferred_element_type=F32)
        g_last = gc[ch * CC + CC - 1: ch * CC + CC, cols]
        kd = (k[rows, cols] * jnp.exp(g_last - gc[rows, cols])).astype(BF16)
        upd = lax.dot_general(kd, v_new, (((0,), (0,)), ((), ())), preferred_element_type=F32)
        states[gi] = st * jnp.exp(g_last) + jnp.where(bd, upd, 0.0)
        o_l.append(o)
    for gi in range(ng):
        state_ref[gi] = states[gi]

    o = jnp.concatenate([jnp.concatenate(o_l[ch * ng:(ch + 1) * ng], axis=1) for ch in range(nch)], axis=0)
    o = o * lax.rsqrt(_segsum(o * o, e) * (1.0 / DV) + EPS) * og_ref[...]
    z = z_ref[...].astype(F32)
    o_ref[...] = (o * (z * jax.nn.sigmoid(z))).astype(o_ref.dtype)


def _deltanet(proj, conv_w, alog, dtb, og, layer, batch, seq, tm=256):
    n = proj.shape[0]
    nt = seq // tm
    hw = H * DK
    e = jnp.asarray(np.kron(np.eye(H), np.ones((DK, DK))), BF16)
    spread = np.zeros((LANE, 2 * hw), np.float32)
    spread[:H, :hw] = np.kron(np.eye(H), np.ones((1, DK)))
    spread[H:2 * H, hw:] = np.kron(np.eye(H), np.ones((1, DK)))
    ex = jnp.asarray(spread, BF16)
    full = lambda shp: _layer_spec(shp, layer)
    const = lambda shp: pl.BlockSpec(shp, lambda b, i: (0, 0))
    blk = lambda unit, w: pl.BlockSpec((tm, w), lambda b, i: (b * nt + i, unit * LANE // w))
    return pl.pallas_call(
        functools.partial(_deltanet_kernel, tm=tm),
        grid=(batch, nt),
        in_specs=[blk(U_QKV, 3 * hw), blk(U_CZ, hw), blk(U_BA, LANE),
                  full((CONV, 3 * hw)), full((1, LANE)), full((1, LANE)), full((1, hw)),
                  const((hw, hw)), const((LANE, 2 * hw))],
        out_specs=pl.BlockSpec((tm, hw), lambda b, i: (b * nt + i, 0)),
        out_shape=jax.ShapeDtypeStruct((n, hw), BF16),
        scratch_shapes=[pltpu.VMEM((8, 3 * hw), F32),
                        pltpu.VMEM((hw // GW, GW, GW), F32)],
        compiler_params=_cparams(("parallel", "arbitrary")),
        name="deltanet",
    )(proj, proj, proj, conv_w, alog, dtb, og, e, ex)


def _mix_ffn_kernel(x_ref, ya_ref, yb_ref, yc_ref, g0_ref, g1_ref, g2_ref, wb_ref, wo_ref, gn_ref,
                    w1_ref, w2_ref, o_ref, a_ref, *, tf):
    merged = None
    for i, (y_ref, g_ref) in enumerate(((ya_ref, g0_ref), (yb_ref, g1_ref), (yc_ref, g2_ref))):
        y = jnp.dot(y_ref[...], wb_ref[i], preferred_element_type=F32)
        term = jax.nn.sigmoid(g_ref[...].astype(F32)) * y
        merged = term if merged is None else merged + term
    x1 = x_ref[...] + jnp.dot(merged.astype(BF16), wo_ref[...], preferred_element_type=F32)
    h = _rms(x1, gn_ref[...]).astype(BF16)
    for f in range(DFF // tf):
        cols = slice(f * tf, (f + 1) * tf)
        a = jnp.dot(h, w1_ref[:, cols], preferred_element_type=F32)
        a_ref[:, cols] = jnp.square(jnp.maximum(a, 0.0)).astype(BF16)
    o_ref[...] = x1 + jnp.dot(a_ref[...], w2_ref[...], preferred_element_type=F32)


def _mix_ffn(x, ya, yb, yc, proj, wb, wo, gn, w1, w2, layer, tm=512, tf=1024):
    n = x.shape[0]
    tok = lambda w: pl.BlockSpec((tm, w), lambda r: (r, 0))
    gate = lambda i: pl.BlockSpec((tm, D), lambda r, i=i: (r, U_GATE // 8 + i))
    res = lambda shp: _layer_spec(shp, layer, single_buffer=True)
    return pl.pallas_call(
        functools.partial(_mix_ffn_kernel, tf=tf),
        grid=(n // tm,),
        in_specs=[tok(D), tok(BW), tok(BW), tok(BW), gate(0), gate(1), gate(2),
                  res((NB, BW, D)), res((D, D)), res((1, D)), res((D, DFF)), res((DFF, D))],
        out_specs=tok(D),
        out_shape=jax.ShapeDtypeStruct((n, D), F32),
        scratch_shapes=[pltpu.VMEM((tm, DFF), BF16)],
        compiler_params=_cparams(("parallel",)),
        name="mix_ffn",
    )(x, ya, yb, yc, proj, proj, proj, wb, wo, gn, w1, w2)


def _pad_heads(w, per_head, first, used):
    depth, kdim, _ = w.shape
    w = w.reshape(depth, kdim, H, per_head)[..., first:first + used]
    w = jnp.pad(w, ((0, 0), (0, 0), (0, 0), (0, LANE - used)))
    return w.reshape(depth, kdim, H * LANE).astype(BF16)


def kernel(x, positions, norm1_g, w_in, sgu_norm_g, w_spatial, b_spatial, q_lat_norm_g, w_q_up,
           kv_lat_norm_g, w_kv_up, q_norm_g, k_norm_g, conv_w, a_log, dt_bias, o_norm_g,
           w_branch, w_out, norm2_g, w_ff1, w_ff2):
    batch, seq, d = x.shape
    depth = w_in.shape[0]
    n = batch * seq
    xt = x.reshape(n, d)
    ctab, stab = _rope_tables(positions.reshape(n, 1))
    rows = lambda a: a.reshape(depth, 1, -1).astype(F32)
    half = ROPE // 2
    w_perm = _permute_w_in(w_in)
    wcat = (w_spatial.reshape(depth, A_G // 2, 2, A_T, A_T).transpose(0, 1, 3, 2, 4)
            .reshape(depth, A_G // 2, A_T, 2 * A_T).astype(F32))
    sgu_bias = jnp.repeat(jnp.swapaxes(b_spatial, 1, 2), A_W // A_G, axis=2).astype(F32)
    wq = _pad_heads(w_q_up, QK, 0, QK)
    wk = _pad_heads(w_kv_up, NOPE + VD, 0, NOPE)
    wv = _pad_heads(w_kv_up, NOPE + VD, NOPE, VD)
    pad_g = lambda g: jnp.pad(
        jnp.stack([g, jnp.concatenate([g[:, :NOPE], g[:, NOPE + half:], g[:, NOPE:NOPE + half]], axis=1)],
                  axis=1).astype(F32), ((0, 0), (0, 0), (0, LANE - QK)))
    qg, kg = pad_g(q_norm_g), pad_g(k_norm_g)
    at_ca = lambda a: jnp.pad(a.astype(F32), ((0, 0), (H, LANE - 2 * H))).reshape(depth, 1, LANE)
    alog, dtb = at_ca(a_log), at_ca(dt_bias)
    og = jnp.tile(o_norm_g.astype(F32), (1, H)).reshape(depth, 1, H * DV)
    conv_f = conv_w.astype(F32)
    wb, wo, w1, w2 = (w.astype(BF16) for w in (w_branch, w_out, w_ff1, w_ff2))
    g1, g2, gs, gq, gkv = (rows(g) for g in (norm1_g, norm2_g, sgu_norm_g, q_lat_norm_g, kv_lat_norm_g))

    for l in range(depth):
        proj = _inproj(xt, g1, w_perm, l)
        y_a = _sgu(proj, gs, wcat, sgu_bias, l)
        q, k, v = _mla_prep(proj, ctab, stab, gq, gkv, wq, wk, wv, qg, kg, l)
        y_b = _attention(q, k, v, batch, seq)
        y_c = _deltanet(proj, conv_f, alog, dtb, og, l, batch, seq)
        xt = _mix_ffn(xt, y_a, y_b, y_c, proj, wb, wo, g2, w1, w2, l)
    return xt.reshape(batch, seq, d)
```

```python
import functools
import math

import numpy as np
import jax
import jax.numpy as jnp
from jax import lax
from jax.experimental import pallas as pl
from jax.experimental.pallas import tpu as pltpu

F32 = jnp.float32
BF16 = jnp.bfloat16

LANE = 128
VMEM_LIMIT = 56 * 1024 * 1024

D = 1024
A_W = 512
A_G = 8
A_T = 128
H = 8
NOPE = 64
ROPE = 32
VD = 64
QK = NOPE + ROPE
Q_LORA = 384
KV_LORA = 256
ROPE_BASE = 10000.0
DK = 64
DV = 64
CONV = 4
CC = 64
SB = 16
NB = 3
BW = 512
DFF = 4 * D
EPS = 1e-6

U_MLA, U_BA, U_CZ, U_QKV, U_AU, U_AV, U_GATE = 0, 6, 8, 12, 24, 28, 32
DP = 56 * LANE

O_AU, O_AV, O_QLAT, O_KVLAT, O_KROPE = 0, 512, 1024, 1408, 1664
O_CQKV, O_CZ, O_CB, O_CA, O_GATES = 1696, 3232, 3744, 3752, 3760


def _permute_w_in(w):
    w = w.astype(BF16)
    sl = lambda off, n: w[..., off:off + n]
    zeros = lambda n: jnp.zeros(w.shape[:-1] + (n,), w.dtype)
    pieces = [
        sl(O_QLAT, Q_LORA), sl(O_KVLAT, KV_LORA),
        zeros(NOPE), sl(O_KROPE, ROPE), zeros(LANE - QK),
        sl(O_CB, H), sl(O_CA, H), zeros((U_CZ - U_BA) * LANE - 2 * H),
        sl(O_CZ, 512), sl(O_CQKV, 1536), sl(O_AU, A_W), sl(O_AV, A_W),
        sl(O_GATES, NB * D),
    ]
    out = jnp.concatenate(pieces, axis=-1).astype(BF16)
    assert out.shape[-1] == DP
    return out


def _cparams(sem):
    return pltpu.CompilerParams(dimension_semantics=sem, vmem_limit_bytes=VMEM_LIMIT)


def _layer_spec(shape, layer, single_buffer=False):
    index_map = lambda *_: (layer,) + (0,) * len(shape)
    if single_buffer:
        return pl.BlockSpec((None,) + shape, index_map, pipeline_mode=pl.Buffered(1))
    return pl.BlockSpec((None,) + shape, index_map)


def _inproj_kernel(x_ref, g_ref, w_ref, o_ref, *, tn):
    x = x_ref[...]
    ms = jnp.mean(x * x, axis=-1, keepdims=True)
    h = (x * lax.rsqrt(ms + EPS) * g_ref[...]).astype(BF16)
    for j in range(w_ref.shape[1] // tn):
        cols = slice(j * tn, (j + 1) * tn)
        o_ref[:, cols] = jnp.dot(h, w_ref[:, cols], preferred_element_type=F32).astype(o_ref.dtype)


def _inproj(x, g, w, layer, tm=512, tn=1024):
    n = x.shape[0]
    dp = w.shape[-1]
    return pl.pallas_call(
        functools.partial(_inproj_kernel, tn=tn),
        grid=(n // tm,),
        in_specs=[pl.BlockSpec((tm, D), lambda i: (i, 0)),
                  _layer_spec((1, D), layer),
                  _layer_spec((D, dp), layer, single_buffer=True)],
        out_specs=pl.BlockSpec((tm, dp), lambda i: (i, 0)),
        out_shape=jax.ShapeDtypeStruct((n, dp), BF16),
        compiler_params=_cparams(("parallel",)),
        name="inproj",
    )(x, g, w)


def _rope_kernel(pos_ref, invf_ref, c_ref, s_ref):
    pos = pos_ref[...].astype(F32)
    ang = pos * invf_ref[...]
    lane = lax.broadcasted_iota(jnp.int32, ang.shape, 1)
    c = jnp.cos(ang)
    s = jnp.sin(ang)
    half = ROPE // 2
    c_ref[...] = jnp.where((lane >= NOPE) & (lane < QK), c, 1.0)
    s_ref[...] = jnp.where(lane < NOPE, 0.0,
                           jnp.where(lane < NOPE + half, -s, jnp.where(lane < QK, s, 0.0)))


def _rope_tables(pos, tm=1024):
    n = pos.shape[0]
    half = ROPE // 2
    lane = np.arange(LANE)
    invf = np.where((lane >= NOPE) & (lane < QK),
                    1.0 / (ROPE_BASE ** (((lane - NOPE) % half).astype(np.float32) / half)), 0.0)
    invf = jnp.asarray(invf, F32).reshape(1, LANE)
    return pl.pallas_call(
        _rope_kernel,
        grid=(n // tm,),
        in_specs=[pl.BlockSpec((tm, 1), lambda i: (i, 0)),
                  pl.BlockSpec((1, LANE), lambda i: (0, 0))],
        out_specs=[pl.BlockSpec((tm, LANE), lambda i: (i, 0)),
                   pl.BlockSpec((tm, LANE), lambda i: (i, 0))],
        out_shape=[jax.ShapeDtypeStruct((n, LANE), F32)] * 2,
        compiler_params=_cparams(("parallel",)),
        name="rope_tables",
    )(pos, invf)


def _gelu(x):
    return 0.5 * x * (1.0 + lax.erf(x * (1.0 / math.sqrt(2.0))))


def _sgu_kernel(u_ref, v_ref, g_ref, w_ref, b_ref, o_ref):
    tm = u_ref.shape[0]
    u = _gelu(u_ref[...].astype(F32))
    v = _gelu(v_ref[...].astype(F32))
    ms = jnp.mean(v * v, axis=-1, keepdims=True)
    v = v * lax.rsqrt(ms + EPS) * g_ref[...]
    lo = (lax.broadcasted_iota(jnp.int32, v.shape, 1) % LANE) < (A_W // A_G)
    v_lo = jnp.where(lo, v, 0.0).astype(BF16)
    v_hi = jnp.where(lo, 0.0, v).astype(BF16)
    r = lax.broadcasted_iota(jnp.int32, (A_T, 2 * A_T), 0)
    c = lax.broadcasted_iota(jnp.int32, (A_T, 2 * A_T), 1)
    causal = (c % A_T) <= r
    for p in range(A_G // 2):
        cols = slice(p * LANE, (p + 1) * LANE)
        w = jnp.where(causal, w_ref[p], 0.0).astype(BF16)
        for ch in range(tm // A_T):
            rows = slice(ch * A_T, (ch + 1) * A_T)
            rhs = jnp.concatenate([v_lo[rows, cols], v_hi[rows, cols]], axis=0)
            sv = jnp.dot(w, rhs, preferred_element_type=F32) + b_ref[:, cols]
            o_ref[rows, cols] = (u[rows, cols] * sv).astype(o_ref.dtype)


def _sgu(proj, g, wcat, bias, layer, tm=2048):
    n = proj.shape[0]
    return pl.pallas_call(
        _sgu_kernel,
        grid=(n // tm,),
        in_specs=[pl.BlockSpec((tm, A_W), lambda i: (i, U_AU // 4)),
                  pl.BlockSpec((tm, A_W), lambda i: (i, U_AV // 4)),
                  _layer_spec((1, A_W), layer),
                  _layer_spec((A_G // 2, A_T, 2 * A_T), layer),
                  _layer_spec((A_T, A_W), layer)],
        out_specs=pl.BlockSpec((tm, A_W), lambda i: (i, 0)),
        out_shape=jax.ShapeDtypeStruct((n, A_W), BF16),
        compiler_params=_cparams(("parallel",)),
        name="sgu",
    )(proj, proj, g, wcat, bias)


def _rms(x, g):
    ms = jnp.mean(x * x, axis=-1, keepdims=True)
    return x * lax.rsqrt(ms + EPS) * g


def _mla_prep_kernel(p_ref, c_ref, s_ref, gq_ref, gkv_ref, wq_ref, wk_ref, wv_ref, qg_ref, kg_ref,
                     q_ref, k_ref, v_ref):
    p = p_ref[...].astype(F32)
    ql = p[:, :Q_LORA]
    kvl = p[:, Q_LORA:Q_LORA + KV_LORA]
    kr = p[:, Q_LORA + KV_LORA:]
    qn = _rms(ql, gq_ref[...]).astype(BF16)
    kvn = _rms(kvl, gkv_ref[...]).astype(BF16)
    q = jnp.dot(qn, wq_ref[...], preferred_element_type=F32)
    kn = jnp.dot(kvn, wk_ref[...], preferred_element_type=F32)
    v = jnp.dot(kvn, wv_ref[...], preferred_element_type=F32)
    vlane = lax.broadcasted_iota(jnp.int32, v.shape, 1) % LANE
    v_ref[...] = jnp.where(vlane == VD, 1.0, v).astype(v_ref.dtype)
    cs = c_ref[...]
    sn = s_ref[...]
    lane = lax.broadcasted_iota(jnp.int32, cs.shape, 1)
    upper = lane >= NOPE + ROPE // 2
    scale = QK ** -0.5 * math.log2(math.e)

    q_tabs = (cs * qg_ref[0:1, :], sn * qg_ref[1:2, :])
    k_tabs = (cs * kg_ref[0:1, :], sn * kg_ref[1:2, :])

    tiles = []
    for h in range(H):
        cols = slice(h * LANE, (h + 1) * LANE)
        tiles.append((q_ref, cols, q[:, cols], q_tabs, scale))
        tiles.append((k_ref, cols, kn[:, cols] + kr, k_tabs, 1.0))
    sums = [jnp.sum(x * x, axis=-1, keepdims=True) for _, _, x, _, _ in tiles]
    rs = [lax.rsqrt(ss * (1.0 / QK) + EPS) * sc for ss, (_, _, _, _, sc) in zip(sums, tiles)]
    rot = [x * tabs[0]
           + jnp.where(upper, pltpu.roll(x, ROPE // 2, 1), pltpu.roll(x, LANE - ROPE // 2, 1)) * tabs[1]
           for _, _, x, tabs, _ in tiles]
    for (ref, cols, _, _, _), y, r in zip(tiles, rot, rs):
        ref[:, cols] = (y * r).astype(ref.dtype)


def _mla_prep(proj, ctab, stab, gq, gkv, wq, wk, wv, qg, kg, layer, tm=1024):
    n = proj.shape[0]
    wm = 6 * LANE
    full = lambda shp: _layer_spec(shp, layer)
    return pl.pallas_call(
        _mla_prep_kernel,
        grid=(n // tm,),
        in_specs=[pl.BlockSpec((tm, wm), lambda i: (i, 0)),
                  pl.BlockSpec((tm, LANE), lambda i: (i, 0)),
                  pl.BlockSpec((tm, LANE), lambda i: (i, 0)),
                  full((1, Q_LORA)), full((1, KV_LORA)),
                  full((Q_LORA, H * LANE)), full((KV_LORA, H * LANE)), full((KV_LORA, H * LANE)),
                  full((2, LANE)), full((2, LANE))],
        out_specs=[pl.BlockSpec((tm, H * LANE), lambda i: (i, 0))] * 3,
        out_shape=[jax.ShapeDtypeStruct((n, H * LANE), BF16)] * 3,
        compiler_params=_cparams(("parallel",)),
        name="mla_prep",
    )(proj, ctab, stab, gq, gkv, wq, wk, wv, qg, kg)


def _attn_kernel(q_ref, k_ref, v_ref, o_ref, m_ref, acc_ref, *, t):
    qi = pl.program_id(2)
    heads = (slice(0, LANE), slice(LANE, 2 * LANE))
    half = t // 2
    m_ref[...] = jnp.full(m_ref.shape, -jnp.inf, F32)
    acc_ref[...] = jnp.zeros(acc_ref.shape, F32)

    def update(groups):
        work = [(h, c) + g for g in groups for h, c in enumerate(heads)]
        ss = []
        for h, c, rows, start, w, shift in work:
            s = lax.dot_general(q_ref[rows, c], k_ref[pl.ds(start, w), c], (((1,), (1,)), ((), ())),
                                preferred_element_type=F32)
            if shift is not None:
                nr = rows.stop - rows.start
                row = lax.broadcasted_iota(jnp.int32, (nr, w), 0)
                col = lax.broadcasted_iota(jnp.int32, (nr, w), 1)
                s = jnp.where(col <= row + shift, s, -jnp.inf)
            ss.append(s)
        m_old = [m_ref[h, rows, :] for h, _, rows, _, _, _ in work]
        m_new = [jnp.maximum(m, jnp.max(s, axis=-1, keepdims=True)) for m, s in zip(m_old, ss)]
        ps = [jnp.exp2(s - m).astype(BF16) for s, m in zip(ss, m_new)]
        pv = [jnp.dot(p, v_ref[pl.ds(start, w), c], preferred_element_type=F32)
              for p, (_, c, _, start, w, _) in zip(ps, work)]
        for (h, _, rows, _, _, _), mo, mn, x in zip(work, m_old, m_new, pv):
            acc_ref[h, rows, :] = jnp.exp2(mo - mn) * acc_ref[h, rows, :] + x
            m_ref[h, rows, :] = mn

    everyone = slice(0, t)

    def pair(j, carry):
        update([(everyone, pl.multiple_of(j * (2 * t), 2 * t), 2 * t, None)])
        return carry

    lax.fori_loop(0, qi // 2, pair, 0)

    @pl.when(qi % 2 == 1)
    def _():
        start, w = pl.multiple_of((qi - 1) * t, 2 * t), 2 * t
        update([(slice(0, half), start, w - half, w - t), (slice(half, t), start, w, w - t + half)])

    @pl.when(qi % 2 == 0)
    def _():
        update([(everyone, pl.multiple_of(qi * t, t), t, 0)])

    acc0 = acc_ref[0]
    acc1 = acc_ref[1]
    out0 = acc0 / acc0[:, VD:VD + 1]
    out1 = acc1 / acc1[:, VD:VD + 1]
    lane = lax.broadcasted_iota(jnp.int32, (t, LANE), 1)
    o_ref[...] = jnp.where(lane < VD, out0, pltpu.roll(out1, VD, 1)).astype(o_ref.dtype)


def _attention(q, k, v, batch, seq, t=1024):
    n = q.shape[0]
    nq = seq // t
    k3 = k.reshape(batch, seq, H * LANE)
    v3 = v.reshape(batch, seq, H * LANE)
    return pl.pallas_call(
        functools.partial(_attn_kernel, t=t),
        grid=(batch, H // 2, nq),
        in_specs=[pl.BlockSpec((t, 2 * LANE), lambda b, hp, i: (b * nq + i, hp)),
                  pl.BlockSpec((None, seq, 2 * LANE), lambda b, hp, i: (b, 0, hp), pipeline_mode=pl.Buffered(1)),
                  pl.BlockSpec((None, seq, 2 * LANE), lambda b, hp, i: (b, 0, hp), pipeline_mode=pl.Buffered(1))],
        out_specs=pl.BlockSpec((t, 2 * VD), lambda b, hp, i: (b * nq + i, hp)),
        out_shape=jax.ShapeDtypeStruct((n, H * VD), BF16),
        scratch_shapes=[pltpu.VMEM((2, t, 1), F32), pltpu.VMEM((2, t, LANE), F32)],
        compiler_params=_cparams(("parallel", "parallel", "arbitrary")),
        name="attention",
    )(q, k3, v3)


GW = 4 * DK


def _blockdiag(x, mask):
    return jnp.where(mask, jnp.concatenate([x] * (GW // CC), axis=0), jnp.zeros((), x.dtype))


def _split3(x):
    hi = x.astype(BF16)
    r = x - hi.astype(F32)
    mid = r.astype(BF16)
    lo = (r - mid.astype(F32)).astype(BF16)
    return hi, mid, lo


def _dot_exact_rhs(sel, x):
    n = x.shape[1]
    r = jnp.dot(sel, jnp.concatenate(_split3(x), axis=1), preferred_element_type=F32)
    return r[:, :n] + r[:, n:2 * n] + r[:, 2 * n:]


def _bd_dot(a, b, mask):
    return jnp.dot(a.astype(BF16), _blockdiag(b.astype(BF16), mask), preferred_element_type=F32)


def _segsum(x, e):
    return jnp.dot(x.astype(BF16), e, preferred_element_type=F32)


def _deltanet_kernel(qkv_ref, z_ref, ba_ref, cw_ref, alog_ref, dtb_ref, og_ref, e_ref, ex_ref,
                     o_ref, tail_ref, state_ref, *, tm):
    t_idx = pl.program_id(1)
    nch = tm // CC
    ng = (H * DK) // GW
    width = 3 * H * DK

    @pl.when(t_idx == 0)
    def _():
        tail_ref[...] = jnp.zeros(tail_ref.shape, F32)
        state_ref[...] = jnp.zeros(state_ref.shape, F32)

    xb = qkv_ref[...]
    x = xb.astype(F32)
    sr = lax.broadcasted_iota(jnp.int32, ((CONV - 1) * tm, tm), 0)
    sc = lax.broadcasted_iota(jnp.int32, ((CONV - 1) * tm, tm), 1)
    sel = jnp.where(sc == (sr % tm) - (CONV - 1 - sr // tm), 1.0, 0.0).astype(BF16)
    shifted = jnp.dot(sel, xb, preferred_element_type=F32)
    xc = cw_ref[CONV - 1:CONV, :] * x
    for j in range(CONV - 1):
        xc = xc + cw_ref[j:j + 1, :] * shifted[j * tm:(j + 1) * tm]
    head = jnp.concatenate([tail_ref[...], x[0:8]], axis=0)
    fix = cw_ref[CONV - 1:CONV, :] * x[0:8]
    for j in range(CONV - 1):
        fix = fix + cw_ref[j:j + 1, :] * head[8 - (CONV - 1) + j: 16 - (CONV - 1) + j]
    xc = jnp.concatenate([fix, xc[8:]], axis=0)
    tail_ref[...] = x[tm - 8:tm]
    xc = xc * jax.nn.sigmoid(xc)
    e = e_ref[...]
    hw = H * DK
    q = xc[:, :hw]
    k = xc[:, hw:2 * hw]
    v = xc[:, 2 * hw:]
    q = q * lax.rsqrt(_segsum(q * q, e) + EPS) * (DK ** -0.5)
    k = k * lax.rsqrt(_segsum(k * k, e) + EPS)
    ba = ba_ref[...].astype(F32)
    sp_in = ba + dtb_ref[...]
    softplus = jnp.maximum(sp_in, 0.0) + jnp.log1p(jnp.exp(-jnp.abs(sp_in)))
    is_b = lax.broadcasted_iota(jnp.int32, ba.shape, 1) < H
    gates = jnp.where(is_b, jax.nn.sigmoid(ba), -jnp.exp(alog_ref[...]) * softplus)
    pieces = jnp.dot(jnp.concatenate(_split3(gates), axis=0), ex_ref[...], preferred_element_type=F32)
    spread = pieces[:tm] + pieces[tm:2 * tm] + pieces[2 * tm:]
    beta = spread[:, :hw]
    g = spread[:, hw:]

    ri = lax.broadcasted_iota(jnp.int32, (tm, tm), 0)
    ci = lax.broadcasted_iota(jnp.int32, (tm, tm), 1)
    cum = jnp.where((ri // CC == ci // CC) & (ci <= ri), 1.0, 0.0).astype(BF16)
    gc = _dot_exact_rhs(cum, g)
    egc = jnp.exp(gc)
    kb = k * beta
    vb = v * beta
    kbg = kb * egc
    qg = q * egc

    r64 = lax.broadcasted_iota(jnp.int32, (CC, GW), 0)
    l64 = lax.broadcasted_iota(jnp.int32, (CC, GW), 1) % CC
    tril = l64 <= r64
    strict = l64 < r64
    eye_t = jnp.where(l64 == r64, 1.0, 0.0).astype(F32)
    rb = lax.broadcasted_iota(jnp.int32, (GW, GW), 0) // CC
    lb = lax.broadcasted_iota(jnp.int32, (GW, GW), 1) // CC
    bd = rb == lb
    ones = jnp.ones((CC, CC), BF16)

    tiles = [(slice(ch * CC, (ch + 1) * CC), slice(gi * GW, (gi + 1) * GW), ch, gi)
             for ch in range(nch) for gi in range(ng)]
    low_l, intra_l = [], []
    for rows, cols, _, _ in tiles:
        gc_c = gc[rows, cols]
        bdk = _blockdiag(k[rows, cols].astype(BF16), bd)
        lhs = jnp.concatenate([kb[rows, cols], q[rows, cols]], axis=0).astype(BF16)
        a2 = lax.dot_general(lhs, bdk, (((1,), (1,)), ((), ())), preferred_element_type=F32)
        gct = _dot_exact_rhs(ones, jnp.where(l64 == r64, gc_c, 0.0))
        decay = jnp.exp(jnp.where(tril, gc_c - gct, -jnp.inf))
        low_l.append(jnp.where(strict, a2[:CC] * decay, 0.0))
        intra_l.append(jnp.where(tril, a2[CC:] * decay, 0.0).astype(BF16))
    n_tiles = len(tiles)
    diag_l = [jnp.where(r64 // SB == l64 // SB, low, 0.0) for low in low_l]
    p_l = [eye_t - d for d in diag_l]
    m_l = [_bd_dot(d, d, bd) for d in diag_l]
    n_sq = int(math.log2(SB)) - 1
    for lvl in range(n_sq):
        for i in range(n_tiles):
            if lvl < n_sq - 1:
                r = _bd_dot(jnp.concatenate([p_l[i], m_l[i]], axis=0), m_l[i], bd)
                p_l[i] = p_l[i] + r[:CC]
                m_l[i] = r[CC:]
            else:
                p_l[i] = p_l[i] + _bd_dot(p_l[i], m_l[i], bd)
    size = SB
    while size < CC:
        off = (r64 // (2 * size) == l64 // (2 * size)) & (r64 // size != l64 // size)
        x_l = [_bd_dot(p, jnp.where(off, low, 0.0), bd) for p, low in zip(p_l, low_l)]
        p_l = [p - _bd_dot(x, p, bd) for p, x in zip(p_l, x_l)]
        size *= 2
    uw_l = []
    for (rows, cols, _, _), p_acc in zip(tiles, p_l):
        rhs = jnp.concatenate([_blockdiag(vb[rows, cols].astype(BF16), bd),
                               _blockdiag(kbg[rows, cols].astype(BF16), bd)], axis=1)
        uw_l.append(jnp.dot(p_acc.astype(BF16), rhs, preferred_element_type=F32))

    states = [state_ref[gi] for gi in range(ng)]
    o_l = []
    for (rows, cols, ch, gi), uw, intra in zip(tiles, uw_l, intra_l):
        st = states[gi]
        lhs2 = jnp.concatenate([uw[:, GW:], qg[rows, cols]], axis=0).astype(BF16)
        ws = jnp.dot(lhs2, st.astype(BF16), preferred_element_type=F32)
        v_new = (uw[:, :GW] - ws[:CC]).astype(BF16)
        o = ws[CC:] + jnp.dot(intra, _blockdiag(v_new, bd), preferred_element_type=F32)
        g_last = gc[ch * CC + CC - 1: ch * CC + CC, cols]
        kd = (k[rows, cols] * jnp.exp(g_last - gc[rows, cols])).astype(BF16)
        upd = lax.dot_general(kd, v_new, (((0,), (0,)), ((), ())), preferred_element_type=F32)
        states[gi] = st * jnp.exp(g_last) + jnp.where(bd, upd, 0.0)
        o_l.append(o)
    for gi in range(ng):
        state_ref[gi] = states[gi]

    o = jnp.concatenate([jnp.concatenate(o_l[ch * ng:(ch + 1) * ng], axis=1) for ch in range(nch)], axis=0)
    o = o * lax.rsqrt(_segsum(o * o, e) * (1.0 / DV) + EPS) * og_ref[...]
    z = z_ref[...].astype(F32)
    o_ref[...] = (o * (z * jax.nn.sigmoid(z))).astype(o_ref.dtype)


def _deltanet(proj, conv_w, alog, dtb, og, layer, batch, seq, tm=256):
    n = proj.shape[0]
    nt = seq // tm
    hw = H * DK
    e = jnp.asarray(np.kron(np.eye(H), np.ones((DK, DK))), BF16)
    spread = np.zeros((LANE, 2 * hw), np.float32)
    spread[:H, :hw] = np.kron(np.eye(H), np.ones((1, DK)))
    spread[H:2 * H, hw:] = np.kron(np.eye(H), np.ones((1, DK)))
    ex = jnp.asarray(spread, BF16)
    full = lambda shp: _layer_spec(shp, layer)
    const = lambda shp: pl.BlockSpec(shp, lambda b, i: (0, 0))
    blk = lambda unit, w: pl.BlockSpec((tm, w), lambda b, i: (b * nt + i, unit * LANE // w))
    return pl.pallas_call(
        functools.partial(_deltanet_kernel, tm=tm),
        grid=(batch, nt),
        in_specs=[blk(U_QKV, 3 * hw), blk(U_CZ, hw), blk(U_BA, LANE),
                  full((CONV, 3 * hw)), full((1, LANE)), full((1, LANE)), full((1, hw)),
                  const((hw, hw)), const((LANE, 2 * hw))],
        out_specs=pl.BlockSpec((tm, hw), lambda b, i: (b * nt + i, 0)),
        out_shape=jax.ShapeDtypeStruct((n, hw), BF16),
        scratch_shapes=[pltpu.VMEM((8, 3 * hw), F32),
                        pltpu.VMEM((hw // GW, GW, GW), F32)],
        compiler_params=_cparams(("parallel", "arbitrary")),
        name="deltanet",
    )(proj, proj, proj, conv_w, alog, dtb, og, e, ex)


def _mix_ffn_kernel(x_ref, ya_ref, yb_ref, yc_ref, g0_ref, g1_ref, g2_ref, wb_ref, wo_ref, gn_ref,
                    w1_ref, w2_ref, o_ref, a_ref, *, tf):
    merged = None
    for i, (y_ref, g_ref) in enumerate(((ya_ref, g0_ref), (yb_ref, g1_ref), (yc_ref, g2_ref))):
        y = jnp.dot(y_ref[...], wb_ref[i], preferred_element_type=F32)
        term = jax.nn.sigmoid(g_ref[...].astype(F32)) * y
        merged = term if merged is None else merged + term
    x1 = x_ref[...] + jnp.dot(merged.astype(BF16), wo_ref[...], preferred_element_type=F32)
    h = _rms(x1, gn_ref[...]).astype(BF16)
    for f in range(DFF // tf):
        cols = slice(f * tf, (f + 1) * tf)
        a = jnp.dot(h, w1_ref[:, cols], preferred_element_type=F32)
        a_ref[:, cols] = jnp.square(jnp.maximum(a, 0.0)).astype(BF16)
    o_ref[...] = x1 + jnp.dot(a_ref[...], w2_ref[...], preferred_element_type=F32)


def _mix_ffn(x, ya, yb, yc, proj, wb, wo, gn, w1, w2, layer, tm=512, tf=1024):
    n = x.shape[0]
    tok = lambda w: pl.BlockSpec((tm, w), lambda r: (r, 0))
    gate = lambda i: pl.BlockSpec((tm, D), lambda r, i=i: (r, U_GATE // 8 + i))
    res = lambda shp: _layer_spec(shp, layer, single_buffer=True)
    return pl.pallas_call(
        functools.partial(_mix_ffn_kernel, tf=tf),
        grid=(n // tm,),
        in_specs=[tok(D), tok(BW), tok(BW), tok(BW), gate(0), gate(1), gate(2),
                  res((NB, BW, D)), res((D, D)), res((1, D)), res((D, DFF)), res((DFF, D))],
        out_specs=tok(D),
        out_shape=jax.ShapeDtypeStruct((n, D), F32),
        scratch_shapes=[pltpu.VMEM((tm, DFF), BF16)],
        compiler_params=_cparams(("parallel",)),
        name="mix_ffn",
    )(x, ya, yb, yc, proj, proj, proj, wb, wo, gn, w1, w2)


def _pad_heads(w, per_head, first, used):
    depth, kdim, _ = w.shape
    w = w.reshape(depth, kdim, H, per_head)[..., first:first + used]
    w = jnp.pad(w, ((0, 0), (0, 0), (0, 0), (0, LANE - used)))
    return w.reshape(depth, kdim, H * LANE).astype(BF16)


def kernel(x, positions, norm1_g, w_in, sgu_norm_g, w_spatial, b_spatial, q_lat_norm_g, w_q_up,
           kv_lat_norm_g, w_kv_up, q_norm_g, k_norm_g, conv_w, a_log, dt_bias, o_norm_g,
           w_branch, w_out, norm2_g, w_ff1, w_ff2):
    batch, seq, d = x.shape
    depth = w_in.shape[0]
    n = batch * seq
    xt = x.reshape(n, d)
    ctab, stab = _rope_tables(positions.reshape(n, 1))
    rows = lambda a: a.reshape(depth, 1, -1).astype(F32)
    half = ROPE // 2
    w_perm = _permute_w_in(w_in)
    wcat = (w_spatial.reshape(depth, A_G // 2, 2, A_T, A_T).transpose(0, 1, 3, 2, 4)
            .reshape(depth, A_G // 2, A_T, 2 * A_T).astype(F32))
    sgu_bias = jnp.repeat(jnp.swapaxes(b_spatial, 1, 2), A_W // A_G, axis=2).astype(F32)
    wq = _pad_heads(w_q_up, QK, 0, QK)
    wk = _pad_heads(w_kv_up, NOPE + VD, 0, NOPE)
    wv = _pad_heads(w_kv_up, NOPE + VD, NOPE, VD)
    pad_g = lambda g: jnp.pad(
        jnp.stack([g, jnp.concatenate([g[:, :NOPE], g[:, NOPE + half:], g[:, NOPE:NOPE + half]], axis=1)],
                  axis=1).astype(F32), ((0, 0), (0, 0), (0, LANE - QK)))
    qg, kg = pad_g(q_norm_g), pad_g(k_norm_g)
    at_ca = lambda a: jnp.pad(a.astype(F32), ((0, 0), (H, LANE - 2 * H))).reshape(depth, 1, LANE)
    alog, dtb = at_ca(a_log), at_ca(dt_bias)
    og = jnp.tile(o_norm_g.astype(F32), (1, H)).reshape(depth, 1, H * DV)
    conv_f = conv_w.astype(F32)
    wb, wo, w1, w2 = (w.astype(BF16) for w in (w_branch, w_out, w_ff1, w_ff2))
    g1, g2, gs, gq, gkv = (rows(g) for g in (norm1_g, norm2_g, sgu_norm_g, q_lat_norm_g, kv_lat_norm_g))

    for l in range(depth):
        proj = _inproj(xt, g1, w_perm, l)
        y_a = _sgu(proj, gs, wcat, sgu_bias, l)
        q, k, v = _mla_prep(proj, ctab, stab, gq, gkv, wq, wk, wv, qg, kg, l)
        y_b = _attention(q, k, v, batch, seq)
        y_c = _deltanet(proj, conv_f, alog, dtb, og, l, batch, seq)
        xt = _mix_ffn(xt, y_a, y_b, y_c, proj, wb, wo, g2, w1, w2, l)
    return xt.reshape(batch, seq, d)
```

```python
import functools
import math

import numpy as np
import jax
import jax.numpy as jnp
from jax import lax
from jax.experimental import pallas as pl
from jax.experimental.pallas import tpu as pltpu

F32 = jnp.float32
BF16 = jnp.bfloat16

LANE = 128
VMEM_LIMIT = 56 * 1024 * 1024

D = 1024
A_W = 512
A_G = 8
A_T = 128
H = 8
NOPE = 64
ROPE = 32
VD = 64
QK = NOPE + ROPE
Q_LORA = 384
KV_LORA = 256
ROPE_BASE = 10000.0
DK = 64
DV = 64
CONV = 4
CC = 64
SB = 16
NB = 3
BW = 512
DFF = 4 * D
EPS = 1e-6

U_MLA, U_BA, U_CZ, U_QKV, U_AU, U_AV, U_GATE = 0, 6, 8, 12, 24, 28, 32
DP = 56 * LANE

O_AU, O_AV, O_QLAT, O_KVLAT, O_KROPE = 0, 512, 1024, 1408, 1664
O_CQKV, O_CZ, O_CB, O_CA, O_GATES = 1696, 3232, 3744, 3752, 3760


def _permute_w_in(w):
    w = w.astype(BF16)
    sl = lambda off, n: w[..., off:off + n]
    zeros = lambda n: jnp.zeros(w.shape[:-1] + (n,), w.dtype)
    pieces = [
        sl(O_QLAT, Q_LORA), sl(O_KVLAT, KV_LORA),
        zeros(NOPE), sl(O_KROPE, ROPE), zeros(LANE - QK),
        sl(O_CB, H), sl(O_CA, H), zeros((U_CZ - U_BA) * LANE - 2 * H),
        sl(O_CZ, 512), sl(O_CQKV, 1536), sl(O_AU, A_W), sl(O_AV, A_W),
        sl(O_GATES, NB * D),
    ]
    out = jnp.concatenate(pieces, axis=-1).astype(BF16)
    assert out.shape[-1] == DP
    return out


def _cparams(sem):
    return pltpu.CompilerParams(dimension_semantics=sem, vmem_limit_bytes=VMEM_LIMIT)


def _layer_spec(shape, layer, single_buffer=False):
    index_map = lambda *_: (layer,) + (0,) * len(shape)
    if single_buffer:
        return pl.BlockSpec((None,) + shape, index_map, pipeline_mode=pl.Buffered(1))
    return pl.BlockSpec((None,) + shape, index_map)


def _rms(x, g):
    ms = jnp.mean(x * x, axis=-1, keepdims=True)
    return x * lax.rsqrt(ms + EPS) * g


def _inproj_mla_kernel(x_ref, g_ref, w_ref, c_ref, s_ref, gq_ref, gkv_ref, wq_ref, wk_ref, wv_ref,
                       qg_ref, kg_ref, gs_ref, ws_ref, bs_ref, o_ref, q_ref, k_ref, v_ref, ya_ref, *, tn):
    h = _rms(x_ref[...], g_ref[...]).astype(BF16)
    assert (U_AU * LANE) % tn == 0 and U_AV * LANE == U_AU * LANE + A_W and tn == 2 * A_W
    sgu_chunk = U_AU * LANE // tn
    pending = [0, sgu_chunk] + [j for j in range(w_ref.shape[1] // tn) if j not in (0, sgu_chunk)]

    def project_chunk():
        if not pending:
            return None
        j = pending.pop(0)
        cols = slice(j * tn, (j + 1) * tn)
        out = jnp.dot(h, w_ref[:, cols], preferred_element_type=F32).astype(o_ref.dtype)
        o_ref[:, cols] = out
        return out

    p = project_chunk()[:, :6 * LANE].astype(F32)
    ql = p[:, :Q_LORA]
    kvl = p[:, Q_LORA:Q_LORA + KV_LORA]
    kr = p[:, Q_LORA + KV_LORA:]
    qn = _rms(ql, gq_ref[...]).astype(BF16)
    kvn = _rms(kvl, gkv_ref[...]).astype(BF16)
    q = jnp.dot(qn, wq_ref[...], preferred_element_type=F32)
    kn = jnp.dot(kvn, wk_ref[...], preferred_element_type=F32)
    v = jnp.dot(kvn, wv_ref[...], preferred_element_type=F32)
    vlane = lax.broadcasted_iota(jnp.int32, v.shape, 1) % LANE
    v_ref[...] = jnp.where(vlane == VD, 1.0, v).astype(v_ref.dtype)
    uv = project_chunk()
    _sgu_body(uv[:, :A_W], uv[:, A_W:], gs_ref, ws_ref, bs_ref, ya_ref)
    cs = c_ref[...]
    sn = s_ref[...]
    lane = lax.broadcasted_iota(jnp.int32, cs.shape, 1)
    upper = lane >= NOPE + ROPE // 2
    scale = QK ** -0.5 * math.log2(math.e)

    q_tabs = (cs * qg_ref[0:1, :], sn * qg_ref[1:2, :])
    k_tabs = (cs * kg_ref[0:1, :], sn * kg_ref[1:2, :])

    tiles = []
    for hd in range(H):
        cols = slice(hd * LANE, (hd + 1) * LANE)
        tiles.append((q_ref, cols, q[:, cols], q_tabs, scale))
        tiles.append((k_ref, cols, kn[:, cols] + kr, k_tabs, 1.0))
    sums = [jnp.sum(x * x, axis=-1, keepdims=True) for _, _, x, _, _ in tiles]
    project_chunk()
    rs = [lax.rsqrt(ss * (1.0 / QK) + EPS) * sc for ss, (_, _, _, _, sc) in zip(sums, tiles)]
    rot = []
    for i, (_, _, x, tabs, _) in enumerate(tiles):
        if i % 4 == 0 and i:
            project_chunk()
        partner = jnp.where(upper, pltpu.roll(x, ROPE // 2, 1), pltpu.roll(x, LANE - ROPE // 2, 1))
        rot.append(x * tabs[0] + partner * tabs[1])
    while pending:
        project_chunk()
    for (ref, cols, _, _, _), y, r in zip(tiles, rot, rs):
        ref[:, cols] = (y * r).astype(ref.dtype)


def _inproj_mla(x, g, w, ctab, stab, gq, gkv, wq, wk, wv, qg, kg, gs, wcat, sgu_bias, layer, tm=512, tn=1024):
    n = x.shape[0]
    dp = w.shape[-1]
    full = lambda shp: _layer_spec(shp, layer)
    tok = lambda width: pl.BlockSpec((tm, width), lambda i: (i, 0))
    return pl.pallas_call(
        functools.partial(_inproj_mla_kernel, tn=tn),
        grid=(n // tm,),
        in_specs=[tok(D), full((1, D)), _layer_spec((D, dp), layer, single_buffer=True),
                  tok(LANE), tok(LANE),
                  full((1, Q_LORA)), full((1, KV_LORA)),
                  full((Q_LORA, H * LANE)), full((KV_LORA, H * LANE)), full((KV_LORA, H * LANE)),
                  full((2, LANE)), full((2, LANE)),
                  full((1, A_W)), full((A_G // 2, A_T, 2 * A_T)), full((A_T, A_W))],
        out_specs=[tok(dp), tok(H * LANE), tok(H * LANE), tok(H * LANE), tok(A_W)],
        out_shape=[jax.ShapeDtypeStruct((n, dp), BF16)] + [jax.ShapeDtypeStruct((n, H * LANE), BF16)] * 3
        + [jax.ShapeDtypeStruct((n, A_W), BF16)],
        compiler_params=_cparams(("parallel",)),
        name="inproj_mla",
    )(x, g, w, ctab, stab, gq, gkv, wq, wk, wv, qg, kg, gs, wcat, sgu_bias)


def _rope_kernel(pos_ref, invf_ref, c_ref, s_ref):
    pos = pos_ref[...].astype(F32)
    ang = pos * invf_ref[...]
    lane = lax.broadcasted_iota(jnp.int32, ang.shape, 1)
    c = jnp.cos(ang)
    s = jnp.sin(ang)
    half = ROPE // 2
    c_ref[...] = jnp.where((lane >= NOPE) & (lane < QK), c, 1.0)
    s_ref[...] = jnp.where(lane < NOPE, 0.0,
                           jnp.where(lane < NOPE + half, -s, jnp.where(lane < QK, s, 0.0)))


def _rope_tables(pos, tm=1024):
    n = pos.shape[0]
    half = ROPE // 2
    lane = np.arange(LANE)
    invf = np.where((lane >= NOPE) & (lane < QK),
                    1.0 / (ROPE_BASE ** (((lane - NOPE) % half).astype(np.float32) / half)), 0.0)
    invf = jnp.asarray(invf, F32).reshape(1, LANE)
    return pl.pallas_call(
        _rope_kernel,
        grid=(n // tm,),
        in_specs=[pl.BlockSpec((tm, 1), lambda i: (i, 0)),
                  pl.BlockSpec((1, LANE), lambda i: (0, 0))],
        out_specs=[pl.BlockSpec((tm, LANE), lambda i: (i, 0)),
                   pl.BlockSpec((tm, LANE), lambda i: (i, 0))],
        out_shape=[jax.ShapeDtypeStruct((n, LANE), F32)] * 2,
        compiler_params=_cparams(("parallel",)),
        name="rope_tables",
    )(pos, invf)


def _gelu(x):
    return 0.5 * x * (1.0 + lax.erf(x * (1.0 / math.sqrt(2.0))))


def _sgu_body(a_u, a_v, g_ref, w_ref, b_ref, o_ref):
    tm = a_u.shape[0]
    u = _gelu(a_u.astype(F32))
    v = _gelu(a_v.astype(F32))
    ms = jnp.mean(v * v, axis=-1, keepdims=True)
    v = v * lax.rsqrt(ms + EPS) * g_ref[...]
    lo = (lax.broadcasted_iota(jnp.int32, v.shape, 1) % LANE) < (A_W // A_G)
    v_lo = jnp.where(lo, v, 0.0).astype(BF16)
    v_hi = jnp.where(lo, 0.0, v).astype(BF16)
    r = lax.broadcasted_iota(jnp.int32, (A_T, 2 * A_T), 0)
    c = lax.broadcasted_iota(jnp.int32, (A_T, 2 * A_T), 1)
    causal = (c % A_T) <= r
    for p in range(A_G // 2):
        cols = slice(p * LANE, (p + 1) * LANE)
        w = jnp.where(causal, w_ref[p], 0.0).astype(BF16)
        for ch in range(tm // A_T):
            rows = slice(ch * A_T, (ch + 1) * A_T)
            rhs = jnp.concatenate([v_lo[rows, cols], v_hi[rows, cols]], axis=0)
            sv = jnp.dot(w, rhs, preferred_element_type=F32) + b_ref[:, cols]
            o_ref[rows, cols] = (u[rows, cols] * sv).astype(o_ref.dtype)


def _attn_kernel(q_ref, k_ref, v_ref, o_ref, *, t):
    qi = pl.program_id(2)
    heads = (slice(0, LANE), slice(LANE, 2 * LANE))
    qs = [q_ref[:, c] for c in heads]

    def step(start, w, carry, masked):
        ss = [lax.dot_general(q, k_ref[pl.ds(start, w), c], (((1,), (1,)), ((), ())),
                              preferred_element_type=F32) for c, q in zip(heads, qs)]
        if masked:
            row = lax.broadcasted_iota(jnp.int32, (t, w), 0)
            col = lax.broadcasted_iota(jnp.int32, (t, w), 1)
            ok = col <= row + (w - t)
            ss = [jnp.where(ok, s, -jnp.inf) for s in ss]
        ms = [jnp.maximum(m, jnp.max(s, axis=-1, keepdims=True)) for s, (m, _) in zip(ss, carry)]
        ps = [jnp.exp2(s - m_new).astype(BF16) for s, m_new in zip(ss, ms)]
        pv = [jnp.dot(p, v_ref[pl.ds(start, w), c], preferred_element_type=F32) for p, c in zip(ps, heads)]
        return tuple((m_new, jnp.exp2(m - m_new) * acc + x) for m_new, (m, acc), x in zip(ms, carry, pv))

    init = tuple((jnp.full((t, 1), -jnp.inf, F32), jnp.zeros((t, LANE), F32)) for _ in heads)
    carry = lax.fori_loop(
        0, qi // 2, lambda j, c: step(pl.multiple_of(j * (2 * t), 2 * t), 2 * t, c, False), init)
    (_, acc0), (_, acc1) = lax.cond(
        qi % 2 == 1,
        lambda c: step(pl.multiple_of((qi - 1) * t, 2 * t), 2 * t, c, True),
        lambda c: step(pl.multiple_of(qi * t, t), t, c, True),
        carry)
    out0 = acc0 / acc0[:, VD:VD + 1]
    out1 = acc1 / acc1[:, VD:VD + 1]
    lane = lax.broadcasted_iota(jnp.int32, (t, LANE), 1)
    o_ref[...] = jnp.where(lane < VD, out0, pltpu.roll(out1, VD, 1)).astype(o_ref.dtype)


def _attention(q, k, v, batch, seq, t=1024):
    n = q.shape[0]
    nq = seq // t
    k3 = k.reshape(batch, seq, H * LANE)
    v3 = v.reshape(batch, seq, H * LANE)
    return pl.pallas_call(
        functools.partial(_attn_kernel, t=t),
        grid=(batch, H // 2, nq),
        in_specs=[pl.BlockSpec((t, 2 * LANE), lambda b, hp, i: (b * nq + i, hp)),
                  pl.BlockSpec((None, seq, 2 * LANE), lambda b, hp, i: (b, 0, hp), pipeline_mode=pl.Buffered(1)),
                  pl.BlockSpec((None, seq, 2 * LANE), lambda b, hp, i: (b, 0, hp), pipeline_mode=pl.Buffered(1))],
        out_specs=pl.BlockSpec((t, 2 * VD), lambda b, hp, i: (b * nq + i, hp)),
        out_shape=jax.ShapeDtypeStruct((n, H * VD), BF16),
        compiler_params=_cparams(("parallel", "parallel", "arbitrary")),
        name="attention",
    )(q, k3, v3)


GW = 4 * DK


def _blockdiag(x, mask):
    return jnp.where(mask, jnp.concatenate([x] * (GW // CC), axis=0), jnp.zeros((), x.dtype))


def _split3(x):
    hi = x.astype(BF16)
    r = x - hi.astype(F32)
    mid = r.astype(BF16)
    lo = (r - mid.astype(F32)).astype(BF16)
    return hi, mid, lo


def _dot_exact_rhs(sel, x):
    n = x.shape[1]
    r = jnp.dot(sel, jnp.concatenate(_split3(x), axis=1), preferred_element_type=F32)
    return r[:, :n] + r[:, n:2 * n] + r[:, 2 * n:]


def _bd_dot(a, b, mask):
    return jnp.dot(a.astype(BF16), _blockdiag(b.astype(BF16), mask), preferred_element_type=F32)


def _segsum(x, e):
    return jnp.dot(x.astype(BF16), e, preferred_element_type=F32)


def _deltanet_kernel(qkv_ref, z_ref, ba_ref, cw_ref, alog_ref, dtb_ref, og_ref, e_ref, ex_ref,
                     o_ref, tail_ref, state_ref, *, tm):
    t_idx = pl.program_id(1)
    nch = tm // CC
    ng = (H * DK) // GW
    width = 3 * H * DK

    @pl.when(t_idx == 0)
    def _():
        tail_ref[...] = jnp.zeros(tail_ref.shape, F32)
        state_ref[...] = jnp.zeros(state_ref.shape, F32)

    xb = qkv_ref[...]
    x = xb.astype(F32)
    sr = lax.broadcasted_iota(jnp.int32, ((CONV - 1) * tm, tm), 0)
    sc = lax.broadcasted_iota(jnp.int32, ((CONV - 1) * tm, tm), 1)
    sel = jnp.where(sc == (sr % tm) - (CONV - 1 - sr // tm), 1.0, 0.0).astype(BF16)
    shifted = jnp.dot(sel, xb, preferred_element_type=F32)
    xc = cw_ref[CONV - 1:CONV, :] * x
    for j in range(CONV - 1):
        xc = xc + cw_ref[j:j + 1, :] * shifted[j * tm:(j + 1) * tm]
    head = jnp.concatenate([tail_ref[...], x[0:8]], axis=0)
    fix = cw_ref[CONV - 1:CONV, :] * x[0:8]
    for j in range(CONV - 1):
        fix = fix + cw_ref[j:j + 1, :] * head[8 - (CONV - 1) + j: 16 - (CONV - 1) + j]
    xc = jnp.concatenate([fix, xc[8:]], axis=0)
    tail_ref[...] = x[tm - 8:tm]
    xc = xc * jax.nn.sigmoid(xc)
    e = e_ref[...]
    hw = H * DK
    q = xc[:, :hw]
    k = xc[:, hw:2 * hw]
    v = xc[:, 2 * hw:]
    q = q * lax.rsqrt(_segsum(q * q, e) + EPS) * (DK ** -0.5)
    k = k * lax.rsqrt(_segsum(k * k, e) + EPS)
    ba = ba_ref[...].astype(F32)
    sp_in = ba + dtb_ref[...]
    softplus = jnp.maximum(sp_in, 0.0) + jnp.log1p(jnp.exp(-jnp.abs(sp_in)))
    is_b = lax.broadcasted_iota(jnp.int32, ba.shape, 1) < H
    gates = jnp.where(is_b, jax.nn.sigmoid(ba), -jnp.exp(alog_ref[...]) * softplus)
    pieces = jnp.dot(jnp.concatenate(_split3(gates), axis=0), ex_ref[...], preferred_element_type=F32)
    spread = pieces[:tm] + pieces[tm:2 * tm] + pieces[2 * tm:]
    beta = spread[:, :hw]
    g = spread[:, hw:]

    ri = lax.broadcasted_iota(jnp.int32, (tm, tm), 0)
    ci = lax.broadcasted_iota(jnp.int32, (tm, tm), 1)
    cum = jnp.where((ri // CC == ci // CC) & (ci <= ri), 1.0, 0.0).astype(BF16)
    gc = _dot_exact_rhs(cum, g)
    egc = jnp.exp(gc)
    kb = k * beta
    vb = v * beta
    kbg = kb * egc
    qg = q * egc

    r64 = lax.broadcasted_iota(jnp.int32, (CC, GW), 0)
    l64 = lax.broadcasted_iota(jnp.int32, (CC, GW), 1) % CC
    tril = l64 <= r64
    strict = l64 < r64
    eye_t = jnp.where(l64 == r64, 1.0, 0.0).astype(F32)
    rb = lax.broadcasted_iota(jnp.int32, (GW, GW), 0) // CC
    lb = lax.broadcasted_iota(jnp.int32, (GW, GW), 1) // CC
    bd = rb == lb
    ones = jnp.ones((CC, CC), BF16)

    tiles = [(slice(ch * CC, (ch + 1) * CC), slice(gi * GW, (gi + 1) * GW), ch, gi)
             for ch in range(nch) for gi in range(ng)]
    low_l, intra_l = [], []
    for rows, cols, _, _ in tiles:
        gc_c = gc[rows, cols]
        bdk = _blockdiag(k[rows, cols].astype(BF16), bd)
        lhs = jnp.concatenate([kb[rows, cols], q[rows, cols]], axis=0).astype(BF16)
        a2 = lax.dot_general(lhs, bdk, (((1,), (1,)), ((), ())), preferred_element_type=F32)
        gct = _dot_exact_rhs(ones, jnp.where(l64 == r64, gc_c, 0.0))
        decay = jnp.exp(jnp.where(tril, gc_c - gct, -jnp.inf))
        low_l.append(jnp.where(strict, a2[:CC] * decay, 0.0))
        intra_l.append(jnp.where(tril, a2[CC:] * decay, 0.0).astype(BF16))
    n_tiles = len(tiles)
    diag_l = [jnp.where(r64 // SB == l64 // SB, low, 0.0) for low in low_l]
    p_l = [eye_t - d for d in diag_l]
    m_l = [_bd_dot(d, d, bd) for d in diag_l]
    n_sq = int(math.log2(SB)) - 1
    for lvl in range(n_sq):
        for i in range(n_tiles):
            if lvl < n_sq - 1:
                r = _bd_dot(jnp.concatenate([p_l[i], m_l[i]], axis=0), m_l[i], bd)
                p_l[i] = p_l[i] + r[:CC]
                m_l[i] = r[CC:]
            else:
                p_l[i] = p_l[i] + _bd_dot(p_l[i], m_l[i], bd)
    size = SB
    while size < CC:
        off = (r64 // (2 * size) == l64 // (2 * size)) & (r64 // size != l64 // size)
        x_l = [_bd_dot(p, jnp.where(off, low, 0.0), bd) for p, low in zip(p_l, low_l)]
        p_l = [p - _bd_dot(x, p, bd) for p, x in zip(p_l, x_l)]
        size *= 2
    uw_l = []
    for (rows, cols, _, _), p_acc in zip(tiles, p_l):
        rhs = jnp.concatenate([_blockdiag(vb[rows, cols].astype(BF16), bd),
                               _blockdiag(kbg[rows, cols].astype(BF16), bd)], axis=1)
        uw_l.append(jnp.dot(p_acc.astype(BF16), rhs, preferred_element_type=F32))

    states = [state_ref[gi] for gi in range(ng)]
    o_l = []
    for (rows, cols, ch, gi), uw, intra in zip(tiles, uw_l, intra_l):
        st = states[gi]
        lhs2 = jnp.concatenate([uw[:, GW:], qg[rows, cols]], axis=0).astype(BF16)
        ws = jnp.dot(lhs2, st.astype(BF16), preferred_element_type=F32)
        v_new = (uw[:, :GW] - ws[:CC]).astype(BF16)
        o = ws[CC:] + jnp.dot(intra, _blockdiag(v_new, bd), preferred_element_type=F32)
        g_last = gc[ch * CC + CC - 1: ch * CC + CC, cols]
        kd = (k[rows, cols] * jnp.exp(g_last - gc[rows, cols])).astype(BF16)
        upd = lax.dot_general(kd, v_new, (((0,), (0,)), ((), ())), preferred_element_type=F32)
        states[gi] = st * jnp.exp(g_last) + jnp.where(bd, upd, 0.0)
        o_l.append(o)
    for gi in range(ng):
        state_ref[gi] = states[gi]

    o = jnp.concatenate([jnp.concatenate(o_l[ch * ng:(ch + 1) * ng], axis=1) for ch in range(nch)], axis=0)
    o = o * lax.rsqrt(_segsum(o * o, e) * (1.0 / DV) + EPS) * og_ref[...]
    z = z_ref[...].astype(F32)
    o_ref[...] = (o * (z * jax.nn.sigmoid(z))).astype(o_ref.dtype)


def _deltanet(proj, conv_w, alog, dtb, og, layer, batch, seq, tm=256):
    n = proj.shape[0]
    nt = seq // tm
    hw = H * DK
    e = jnp.asarray(np.kron(np.eye(H), np.ones((DK, DK))), BF16)
    spread = np.zeros((LANE, 2 * hw), np.float32)
    spread[:H, :hw] = np.kron(np.eye(H), np.ones((1, DK)))
    spread[H:2 * H, hw:] = np.kron(np.eye(H), np.ones((1, DK)))
    ex = jnp.asarray(spread, BF16)
    full = lambda shp: _layer_spec(shp, layer)
    const = lambda shp: pl.BlockSpec(shp, lambda b, i: (0, 0))
    blk = lambda unit, w: pl.BlockSpec((tm, w), lambda b, i: (b * nt + i, unit * LANE // w))
    return pl.pallas_call(
        functools.partial(_deltanet_kernel, tm=tm),
        grid=(batch, nt),
        in_specs=[blk(U_QKV, 3 * hw), blk(U_CZ, hw), blk(U_BA, LANE),
                  full((CONV, 3 * hw)), full((1, LANE)), full((1, LANE)), full((1, hw)),
                  const((hw, hw)), const((LANE, 2 * hw))],
        out_specs=pl.BlockSpec((tm, hw), lambda b, i: (b * nt + i, 0)),
        out_shape=jax.ShapeDtypeStruct((n, hw), BF16),
        scratch_shapes=[pltpu.VMEM((8, 3 * hw), F32),
                        pltpu.VMEM((hw // GW, GW, GW), F32)],
        compiler_params=_cparams(("parallel", "arbitrary")),
        name="deltanet",
    )(proj, proj, proj, conv_w, alog, dtb, og, e, ex)


def _mix_ffn_kernel(x_ref, ya_ref, yb_ref, yc_ref, g0_ref, g1_ref, g2_ref, wb_ref, wo_ref, gn_ref,
                    w1_ref, w2_ref, o_ref, a_ref, *, tf):
    merged = None
    for i, (y_ref, g_ref) in enumerate(((ya_ref, g0_ref), (yb_ref, g1_ref), (yc_ref, g2_ref))):
        y = jnp.dot(y_ref[...], wb_ref[i], preferred_element_type=F32)
        term = jax.nn.sigmoid(g_ref[...].astype(F32)) * y
        merged = term if merged is None else merged + term
    x1 = x_ref[...] + jnp.dot(merged.astype(BF16), wo_ref[...], preferred_element_type=F32)
    h = _rms(x1, gn_ref[...]).astype(BF16)
    for f in range(DFF // tf):
        cols = slice(f * tf, (f + 1) * tf)
        a = jnp.dot(h, w1_ref[:, cols], preferred_element_type=F32)
        a_ref[:, cols] = jnp.square(jnp.maximum(a, 0.0)).astype(BF16)
    o_ref[...] = x1 + jnp.dot(a_ref[...], w2_ref[...], preferred_element_type=F32)


def _mix_ffn(x, ya, yb, yc, proj, wb, wo, gn, w1, w2, layer, tm=512, tf=1024):
    n = x.shape[0]
    tok = lambda w: pl.BlockSpec((tm, w), lambda r: (r, 0))
    gate = lambda i: pl.BlockSpec((tm, D), lambda r, i=i: (r, U_GATE // 8 + i))
    res = lambda shp: _layer_spec(shp, layer, single_buffer=True)
    return pl.pallas_call(
        functools.partial(_mix_ffn_kernel, tf=tf),
        grid=(n // tm,),
        in_specs=[tok(D), tok(BW), tok(BW), tok(BW), gate(0), gate(1), gate(2),
                  res((NB, BW, D)), res((D, D)), res((1, D)), res((D, DFF)), res((DFF, D))],
        out_specs=tok(D),
        out_shape=jax.ShapeDtypeStruct((n, D), F32),
        scratch_shapes=[pltpu.VMEM((tm, DFF), BF16)],
        compiler_params=_cparams(("parallel",)),
        name="mix_ffn",
    )(x, ya, yb, yc, proj, proj, proj, wb, wo, gn, w1, w2)


def _pad_heads(w, per_head, first, used):
    depth, kdim, _ = w.shape
    w = w.reshape(depth, kdim, H, per_head)[..., first:first + used]
    w = jnp.pad(w, ((0, 0), (0, 0), (0, 0), (0, LANE - used)))
    return w.reshape(depth, kdim, H * LANE).astype(BF16)


def kernel(x, positions, norm1_g, w_in, sgu_norm_g, w_spatial, b_spatial, q_lat_norm_g, w_q_up,
           kv_lat_norm_g, w_kv_up, q_norm_g, k_norm_g, conv_w, a_log, dt_bias, o_norm_g,
           w_branch, w_out, norm2_g, w_ff1, w_ff2):
    batch, seq, d = x.shape
    depth = w_in.shape[0]
    n = batch * seq
    xt = x.reshape(n, d)
    ctab, stab = _rope_tables(positions.reshape(n, 1))
    rows = lambda a: a.reshape(depth, 1, -1).astype(F32)
    half = ROPE // 2
    w_perm = _permute_w_in(w_in)
    wcat = (w_spatial.reshape(depth, A_G // 2, 2, A_T, A_T).transpose(0, 1, 3, 2, 4)
            .reshape(depth, A_G // 2, A_T, 2 * A_T).astype(F32))
    sgu_bias = jnp.repeat(jnp.swapaxes(b_spatial, 1, 2), A_W // A_G, axis=2).astype(F32)
    wq = _pad_heads(w_q_up, QK, 0, QK)
    wk = _pad_heads(w_kv_up, NOPE + VD, 0, NOPE)
    wv = _pad_heads(w_kv_up, NOPE + VD, NOPE, VD)
    pad_g = lambda g: jnp.pad(
        jnp.stack([g, jnp.concatenate([g[:, :NOPE], g[:, NOPE + half:], g[:, NOPE:NOPE + half]], axis=1)],
                  axis=1).astype(F32), ((0, 0), (0, 0), (0, LANE - QK)))
    qg, kg = pad_g(q_norm_g), pad_g(k_norm_g)
    at_ca = lambda a: jnp.pad(a.astype(F32), ((0, 0), (H, LANE - 2 * H))).reshape(depth, 1, LANE)
    alog, dtb = at_ca(a_log), at_ca(dt_bias)
    og = jnp.tile(o_norm_g.astype(F32), (1, H)).reshape(depth, 1, H * DV)
    conv_f = conv_w.astype(F32)
    wb, wo, w1, w2 = (w.astype(BF16) for w in (w_branch, w_out, w_ff1, w_ff2))
    g1, g2, gs, gq, gkv = (rows(g) for g in (norm1_g, norm2_g, sgu_norm_g, q_lat_norm_g, kv_lat_norm_g))

    for l in range(depth):
        proj, q, k, v, y_a = _inproj_mla(xt, g1, w_perm, ctab, stab, gq, gkv, wq, wk, wv, qg, kg,
                                         gs, wcat, sgu_bias, l)
        y_b = _attention(q, k, v, batch, seq)
        y_c = _deltanet(proj, conv_f, alog, dtb, og, l, batch, seq)
        xt = _mix_ffn(xt, y_a, y_b, y_c, proj, wb, wo, g2, w1, w2, l)
    return xt.reshape(batch, seq, d)
```

```python
import functools
import math

import numpy as np
import jax
import jax.numpy as jnp
from jax import lax
from jax.experimental import pallas as pl
from jax.experimental.pallas import tpu as pltpu

F32 = jnp.float32
BF16 = jnp.bfloat16

LANE = 128
VMEM_LIMIT = 56 * 1024 * 1024

D = 1024
A_W = 512
A_G = 8
A_T = 128
H = 8
NOPE = 64
ROPE = 32
VD = 64
QK = NOPE + ROPE
Q_LORA = 384
KV_LORA = 256
ROPE_BASE = 10000.0
DK = 64
DV = 64
CONV = 4
CC = 64
SB = 16
NB = 3
BW = 512
DFF = 4 * D
EPS = 1e-6

U_MLA, U_BA, U_CZ, U_QKV, U_AU, U_AV, U_GATE = 0, 6, 8, 12, 24, 28, 32
DP = 56 * LANE

O_AU, O_AV, O_QLAT, O_KVLAT, O_KROPE = 0, 512, 1024, 1408, 1664
O_CQKV, O_CZ, O_CB, O_CA, O_GATES = 1696, 3232, 3744, 3752, 3760


def _permute_w_in(w):
    w = w.astype(BF16)
    sl = lambda off, n: w[..., off:off + n]
    zeros = lambda n: jnp.zeros(w.shape[:-1] + (n,), w.dtype)
    pieces = [
        sl(O_QLAT, Q_LORA), sl(O_KVLAT, KV_LORA),
        zeros(NOPE), sl(O_KROPE, ROPE), zeros(LANE - QK),
        sl(O_CB, H), sl(O_CA, H), zeros((U_CZ - U_BA) * LANE - 2 * H),
        sl(O_CZ, 512), sl(O_CQKV, 1536), sl(O_AU, A_W), sl(O_AV, A_W),
        sl(O_GATES, NB * D),
    ]
    out = jnp.concatenate(pieces, axis=-1).astype(BF16)
    assert out.shape[-1] == DP
    return out


def _cparams(sem):
    return pltpu.CompilerParams(dimension_semantics=sem, vmem_limit_bytes=VMEM_LIMIT)


def _layer_spec(shape, layer, single_buffer=False):
    index_map = lambda *_: (layer,) + (0,) * len(shape)
    if single_buffer:
        return pl.BlockSpec((None,) + shape, index_map, pipeline_mode=pl.Buffered(1))
    return pl.BlockSpec((None,) + shape, index_map)


def _rms(x, g):
    ms = jnp.mean(x * x, axis=-1, keepdims=True)
    return x * lax.rsqrt(ms + EPS) * g


def _inproj_mla_kernel(x_ref, g_ref, w_ref, c_ref, s_ref, gq_ref, gkv_ref, wq_ref, wk_ref, wv_ref,
                       qg_ref, kg_ref, gs_ref, ws_ref, bs_ref, o_ref, q_ref, k_ref, v_ref, ya_ref, *, tn):
    h = _rms(x_ref[...], g_ref[...]).astype(BF16)
    assert (U_AU * LANE) % tn == 0 and U_AV * LANE == U_AU * LANE + A_W and tn == 2 * A_W
    sgu_chunk = U_AU * LANE // tn
    pending = [0, sgu_chunk] + [j for j in range(w_ref.shape[1] // tn) if j not in (0, sgu_chunk)]

    def project_chunk():
        if not pending:
            return None
        j = pending.pop(0)
        cols = slice(j * tn, (j + 1) * tn)
        out = jnp.dot(h, w_ref[:, cols], preferred_element_type=F32).astype(o_ref.dtype)
        o_ref[:, cols] = out
        return out

    p = project_chunk()[:, :6 * LANE].astype(F32)
    ql = p[:, :Q_LORA]
    kvl = p[:, Q_LORA:Q_LORA + KV_LORA]
    kr = p[:, Q_LORA + KV_LORA:]
    qn = _rms(ql, gq_ref[...]).astype(BF16)
    kvn = _rms(kvl, gkv_ref[...]).astype(BF16)
    q = jnp.dot(qn, wq_ref[...], preferred_element_type=F32)
    kn = jnp.dot(kvn, wk_ref[...], preferred_element_type=F32)
    v = jnp.dot(kvn, wv_ref[...], preferred_element_type=F32)
    vlane = lax.broadcasted_iota(jnp.int32, v.shape, 1) % LANE
    v_ref[...] = jnp.where(vlane == VD, 1.0, v).astype(v_ref.dtype)
    uv = project_chunk()
    _sgu_body(uv[:, :A_W], uv[:, A_W:], gs_ref, ws_ref, bs_ref, ya_ref)
    cs = c_ref[...]
    sn = s_ref[...]
    lane = lax.broadcasted_iota(jnp.int32, cs.shape, 1)
    upper = lane >= NOPE + ROPE // 2
    scale = QK ** -0.5 * math.log2(math.e)

    q_tabs = (cs * qg_ref[0:1, :], sn * qg_ref[1:2, :])
    k_tabs = (cs * kg_ref[0:1, :], sn * kg_ref[1:2, :])

    tiles = []
    for hd in range(H):
        cols = slice(hd * LANE, (hd + 1) * LANE)
        tiles.append((q_ref, cols, q[:, cols], q_tabs, scale))
        tiles.append((k_ref, cols, kn[:, cols] + kr, k_tabs, 1.0))
    sums = [jnp.sum(x * x, axis=-1, keepdims=True) for _, _, x, _, _ in tiles]
    project_chunk()
    rs = [lax.rsqrt(ss * (1.0 / QK) + EPS) * sc for ss, (_, _, _, _, sc) in zip(sums, tiles)]
    rot = []
    for i, (_, _, x, tabs, _) in enumerate(tiles):
        if i % 4 == 0 and i:
            project_chunk()
        partner = jnp.where(upper, pltpu.roll(x, ROPE // 2, 1), pltpu.roll(x, LANE - ROPE // 2, 1))
        rot.append(x * tabs[0] + partner * tabs[1])
    while pending:
        project_chunk()
    for (ref, cols, _, _, _), y, r in zip(tiles, rot, rs):
        ref[:, cols] = (y * r).astype(ref.dtype)


def _inproj_mla(x, g, w, ctab, stab, gq, gkv, wq, wk, wv, qg, kg, gs, wcat, sgu_bias, layer, tm=512, tn=1024):
    n = x.shape[0]
    dp = w.shape[-1]
    full = lambda shp: _layer_spec(shp, layer)
    tok = lambda width: pl.BlockSpec((tm, width), lambda i: (i, 0))
    return pl.pallas_call(
        functools.partial(_inproj_mla_kernel, tn=tn),
        grid=(n // tm,),
        in_specs=[tok(D), full((1, D)), _layer_spec((D, dp), layer, single_buffer=True),
                  tok(LANE), tok(LANE),
                  full((1, Q_LORA)), full((1, KV_LORA)),
                  full((Q_LORA, H * LANE)), full((KV_LORA, H * LANE)), full((KV_LORA, H * LANE)),
                  full((2, LANE)), full((2, LANE)),
                  full((1, A_W)), full((A_G // 2, A_T, 2 * A_T)), full((A_T, A_W))],
        out_specs=[tok(dp), tok(H * LANE), tok(H * LANE), tok(H * LANE), tok(A_W)],
        out_shape=[jax.ShapeDtypeStruct((n, dp), BF16)] + [jax.ShapeDtypeStruct((n, H * LANE), BF16)] * 3
        + [jax.ShapeDtypeStruct((n, A_W), BF16)],
        compiler_params=_cparams(("parallel",)),
        name="inproj_mla",
    )(x, g, w, ctab, stab, gq, gkv, wq, wk, wv, qg, kg, gs, wcat, sgu_bias)


def _rope_kernel(pos_ref, invf_ref, c_ref, s_ref):
    pos = pos_ref[...].astype(F32)
    ang = pos * invf_ref[...]
    lane = lax.broadcasted_iota(jnp.int32, ang.shape, 1)
    c = jnp.cos(ang)
    s = jnp.sin(ang)
    half = ROPE // 2
    c_ref[...] = jnp.where((lane >= NOPE) & (lane < QK), c, 1.0)
    s_ref[...] = jnp.where(lane < NOPE, 0.0,
                           jnp.where(lane < NOPE + half, -s, jnp.where(lane < QK, s, 0.0)))


def _rope_tables(pos, tm=1024):
    n = pos.shape[0]
    half = ROPE // 2
    lane = np.arange(LANE)
    invf = np.where((lane >= NOPE) & (lane < QK),
                    1.0 / (ROPE_BASE ** (((lane - NOPE) % half).astype(np.float32) / half)), 0.0)
    invf = jnp.asarray(invf, F32).reshape(1, LANE)
    return pl.pallas_call(
        _rope_kernel,
        grid=(n // tm,),
        in_specs=[pl.BlockSpec((tm, 1), lambda i: (i, 0)),
                  pl.BlockSpec((1, LANE), lambda i: (0, 0))],
        out_specs=[pl.BlockSpec((tm, LANE), lambda i: (i, 0)),
                   pl.BlockSpec((tm, LANE), lambda i: (i, 0))],
        out_shape=[jax.ShapeDtypeStruct((n, LANE), F32)] * 2,
        compiler_params=_cparams(("parallel",)),
        name="rope_tables",
    )(pos, invf)


def _gelu(x):
    return 0.5 * x * (1.0 + lax.erf(x * (1.0 / math.sqrt(2.0))))


def _sgu_body(a_u, a_v, g_ref, w_ref, b_ref, o_ref):
    tm = a_u.shape[0]
    u = _gelu(a_u.astype(F32))
    v = _gelu(a_v.astype(F32))
    ms = jnp.mean(v * v, axis=-1, keepdims=True)
    v = v * lax.rsqrt(ms + EPS) * g_ref[...]
    lo = (lax.broadcasted_iota(jnp.int32, v.shape, 1) % LANE) < (A_W // A_G)
    v_lo = jnp.where(lo, v, 0.0).astype(BF16)
    v_hi = jnp.where(lo, 0.0, v).astype(BF16)
    r = lax.broadcasted_iota(jnp.int32, (A_T, 2 * A_T), 0)
    c = lax.broadcasted_iota(jnp.int32, (A_T, 2 * A_T), 1)
    causal = (c % A_T) <= r
    for p in range(A_G // 2):
        cols = slice(p * LANE, (p + 1) * LANE)
        w = jnp.where(causal, w_ref[p], 0.0).astype(BF16)
        for ch in range(tm // A_T):
            rows = slice(ch * A_T, (ch + 1) * A_T)
            rhs = jnp.concatenate([v_lo[rows, cols], v_hi[rows, cols]], axis=0)
            sv = jnp.dot(w, rhs, preferred_element_type=F32) + b_ref[:, cols]
            o_ref[rows, cols] = (u[rows, cols] * sv).astype(o_ref.dtype)


def _attn_kernel(q_ref, k_ref, v_ref, o_ref, *, t):
    qi = pl.program_id(2)
    heads = (slice(0, LANE), slice(LANE, 2 * LANE))
    qs = [q_ref[:, c] for c in heads]

    def step(start, w, carry, masked):
        ss = [lax.dot_general(q, k_ref[pl.ds(start, w), c], (((1,), (1,)), ((), ())),
                              preferred_element_type=F32) for c, q in zip(heads, qs)]
        if masked:
            row = lax.broadcasted_iota(jnp.int32, (t, w), 0)
            col = lax.broadcasted_iota(jnp.int32, (t, w), 1)
            ok = col <= row + (w - t)
            ss = [jnp.where(ok, s, -jnp.inf) for s in ss]
        ms = [jnp.maximum(m, jnp.max(s, axis=-1, keepdims=True)) for s, (m, _) in zip(ss, carry)]
        ps = [jnp.exp2(s - m_new).astype(BF16) for s, m_new in zip(ss, ms)]
        pv = [jnp.dot(p, v_ref[pl.ds(start, w), c], preferred_element_type=F32) for p, c in zip(ps, heads)]
        return tuple((m_new, jnp.exp2(m - m_new) * acc + x) for m_new, (m, acc), x in zip(ms, carry, pv))

    init = tuple((jnp.full((t, 1), -jnp.inf, F32), jnp.zeros((t, LANE), F32)) for _ in heads)
    carry = lax.fori_loop(
        0, qi // 2, lambda j, c: step(pl.multiple_of(j * (2 * t), 2 * t), 2 * t, c, False), init)
    (_, acc0), (_, acc1) = lax.cond(
        qi % 2 == 1,
        lambda c: step(pl.multiple_of((qi - 1) * t, 2 * t), 2 * t, c, True),
        lambda c: step(pl.multiple_of(qi * t, t), t, c, True),
        carry)
    out0 = acc0 / acc0[:, VD:VD + 1]
    out1 = acc1 / acc1[:, VD:VD + 1]
    lane = lax.broadcasted_iota(jnp.int32, (t, LANE), 1)
    o_ref[...] = jnp.where(lane < VD, out0, pltpu.roll(out1, VD, 1)).astype(o_ref.dtype)


def _attention(q, k, v, batch, seq, t=1024):
    n = q.shape[0]
    nq = seq // t
    k3 = k.reshape(batch, seq, H * LANE)
    v3 = v.reshape(batch, seq, H * LANE)
    return pl.pallas_call(
        functools.partial(_attn_kernel, t=t),
        grid=(batch, H // 2, nq),
        in_specs=[pl.BlockSpec((t, 2 * LANE), lambda b, hp, i: (b * nq + i, hp)),
                  pl.BlockSpec((None, seq, 2 * LANE), lambda b, hp, i: (b, 0, hp)),
                  pl.BlockSpec((None, seq, 2 * LANE), lambda b, hp, i: (b, 0, hp))],
        out_specs=pl.BlockSpec((t, 2 * VD), lambda b, hp, i: (b * nq + i, hp)),
        out_shape=jax.ShapeDtypeStruct((n, H * VD), BF16),
        compiler_params=_cparams(("parallel", "parallel", "arbitrary")),
        name="attention",
    )(q, k3, v3)


GW = 4 * DK


def _blockdiag(x, mask):
    return jnp.where(mask, jnp.concatenate([x] * (GW // CC), axis=0), jnp.zeros((), x.dtype))


def _split3(x):
    hi = x.astype(BF16)
    r = x - hi.astype(F32)
    mid = r.astype(BF16)
    lo = (r - mid.astype(F32)).astype(BF16)
    return hi, mid, lo


def _dot_exact_rhs(sel, x):
    n = x.shape[1]
    r = jnp.dot(sel, jnp.concatenate(_split3(x), axis=1), preferred_element_type=F32)
    return r[:, :n] + r[:, n:2 * n] + r[:, 2 * n:]


def _bd_dot(a, b, mask):
    return jnp.dot(a.astype(BF16), _blockdiag(b.astype(BF16), mask), preferred_element_type=F32)


def _segsum(x, e):
    return jnp.dot(x.astype(BF16), e, preferred_element_type=F32)


def _deltanet_kernel(qkv_ref, z_ref, ba_ref, cw_ref, alog_ref, dtb_ref, og_ref, e_ref, ex_ref,
                     o_ref, tail_ref, state_ref, *, tm):
    t_idx = pl.program_id(1)
    nch = tm // CC
    ng = (H * DK) // GW
    width = 3 * H * DK

    @pl.when(t_idx == 0)
    def _():
        tail_ref[...] = jnp.zeros(tail_ref.shape, F32)
        state_ref[...] = jnp.zeros(state_ref.shape, F32)

    xb = qkv_ref[...]
    x = xb.astype(F32)
    sr = lax.broadcasted_iota(jnp.int32, ((CONV - 1) * tm, tm), 0)
    sc = lax.broadcasted_iota(jnp.int32, ((CONV - 1) * tm, tm), 1)
    sel = jnp.where(sc == (sr % tm) - (CONV - 1 - sr // tm), 1.0, 0.0).astype(BF16)
    shifted = jnp.dot(sel, xb, preferred_element_type=F32)
    xc = cw_ref[CONV - 1:CONV, :] * x
    for j in range(CONV - 1):
        xc = xc + cw_ref[j:j + 1, :] * shifted[j * tm:(j + 1) * tm]
    head = jnp.concatenate([tail_ref[...], x[0:8]], axis=0)
    fix = cw_ref[CONV - 1:CONV, :] * x[0:8]
    for j in range(CONV - 1):
        fix = fix + cw_ref[j:j + 1, :] * head[8 - (CONV - 1) + j: 16 - (CONV - 1) + j]
    xc = jnp.concatenate([fix, xc[8:]], axis=0)
    tail_ref[...] = x[tm - 8:tm]
    xc = xc * jax.nn.sigmoid(xc)
    e = e_ref[...]
    hw = H * DK
    q = xc[:, :hw]
    k = xc[:, hw:2 * hw]
    v = xc[:, 2 * hw:]
    q = q * lax.rsqrt(_segsum(q * q, e) + EPS) * (DK ** -0.5)
    k = k * lax.rsqrt(_segsum(k * k, e) + EPS)
    ba = ba_ref[...].astype(F32)
    sp_in = ba + dtb_ref[...]
    softplus = jnp.maximum(sp_in, 0.0) + jnp.log1p(jnp.exp(-jnp.abs(sp_in)))
    is_b = lax.broadcasted_iota(jnp.int32, ba.shape, 1) < H
    gates = jnp.where(is_b, jax.nn.sigmoid(ba), -jnp.exp(alog_ref[...]) * softplus)
    pieces = jnp.dot(jnp.concatenate(_split3(gates), axis=0), ex_ref[...], preferred_element_type=F32)
    spread = pieces[:tm] + pieces[tm:2 * tm] + pieces[2 * tm:]
    beta = spread[:, :hw]
    g = spread[:, hw:]

    ri = lax.broadcasted_iota(jnp.int32, (tm, tm), 0)
    ci = lax.broadcasted_iota(jnp.int32, (tm, tm), 1)
    cum = jnp.where((ri // CC == ci // CC) & (ci <= ri), 1.0, 0.0).astype(BF16)
    gc = _dot_exact_rhs(cum, g)
    egc = jnp.exp(gc)
    kb = k * beta
    vb = v * beta
    kbg = kb * egc
    qg = q * egc

    r64 = lax.broadcasted_iota(jnp.int32, (CC, GW), 0)
    l64 = lax.broadcasted_iota(jnp.int32, (CC, GW), 1) % CC
    tril = l64 <= r64
    strict = l64 < r64
    eye_t = jnp.where(l64 == r64, 1.0, 0.0).astype(F32)
    rb = lax.broadcasted_iota(jnp.int32, (GW, GW), 0) // CC
    lb = lax.broadcasted_iota(jnp.int32, (GW, GW), 1) // CC
    bd = rb == lb
    ones = jnp.ones((CC, CC), BF16)

    tiles = [(slice(ch * CC, (ch + 1) * CC), slice(gi * GW, (gi + 1) * GW), ch, gi)
             for ch in range(nch) for gi in range(ng)]
    low_l, intra_l = [], []
    for rows, cols, _, _ in tiles:
        gc_c = gc[rows, cols]
        bdk = _blockdiag(k[rows, cols].astype(BF16), bd)
        lhs = jnp.concatenate([kb[rows, cols], q[rows, cols]], axis=0).astype(BF16)
        a2 = lax.dot_general(lhs, bdk, (((1,), (1,)), ((), ())), preferred_element_type=F32)
        gct = _dot_exact_rhs(ones, jnp.where(l64 == r64, gc_c, 0.0))
        decay = jnp.exp(jnp.where(tril, gc_c - gct, -jnp.inf))
        low_l.append(jnp.where(strict, a2[:CC] * decay, 0.0))
        intra_l.append(jnp.where(tril, a2[CC:] * decay, 0.0).astype(BF16))
    n_tiles = len(tiles)
    diag_l = [jnp.where(r64 // SB == l64 // SB, low, 0.0) for low in low_l]
    p_l = [eye_t - d for d in diag_l]
    m_l = [_bd_dot(d, d, bd) for d in diag_l]
    n_sq = int(math.log2(SB)) - 1
    for lvl in range(n_sq):
        for i in range(n_tiles):
            if lvl < n_sq - 1:
                r = _bd_dot(jnp.concatenate([p_l[i], m_l[i]], axis=0), m_l[i], bd)
                p_l[i] = p_l[i] + r[:CC]
                m_l[i] = r[CC:]
            else:
                p_l[i] = p_l[i] + _bd_dot(p_l[i], m_l[i], bd)
    size = SB
    while size < CC:
        off = (r64 // (2 * size) == l64 // (2 * size)) & (r64 // size != l64 // size)
        x_l = [_bd_dot(p, jnp.where(off, low, 0.0), bd) for p, low in zip(p_l, low_l)]
        p_l = [p - _bd_dot(x, p, bd) for p, x in zip(p_l, x_l)]
        size *= 2
    uw_l = []
    for (rows, cols, _, _), p_acc in zip(tiles, p_l):
        rhs = jnp.concatenate([_blockdiag(vb[rows, cols].astype(BF16), bd),
                               _blockdiag(kbg[rows, cols].astype(BF16), bd)], axis=1)
        uw_l.append(jnp.dot(p_acc.astype(BF16), rhs, preferred_element_type=F32))

    states = [state_ref[gi] for gi in range(ng)]
    o_l = []
    for (rows, cols, ch, gi), uw, intra in zip(tiles, uw_l, intra_l):
        st = states[gi]
        lhs2 = jnp.concatenate([uw[:, GW:], qg[rows, cols]], axis=0).astype(BF16)
        ws = jnp.dot(lhs2, st.astype(BF16), preferred_element_type=F32)
        v_new = (uw[:, :GW] - ws[:CC]).astype(BF16)
        o = ws[CC:] + jnp.dot(intra, _blockdiag(v_new, bd), preferred_element_type=F32)
        g_last = gc[ch * CC + CC - 1: ch * CC + CC, cols]
        kd = (k[rows, cols] * jnp.exp(g_last - gc[rows, cols])).astype(BF16)
        upd = lax.dot_general(kd, v_new, (((0,), (0,)), ((), ())), preferred_element_type=F32)
        states[gi] = st * jnp.exp(g_last) + jnp.where(bd, upd, 0.0)
        o_l.append(o)
    for gi in range(ng):
        state_ref[gi] = states[gi]

    o = jnp.concatenate([jnp.concatenate(o_l[ch * ng:(ch + 1) * ng], axis=1) for ch in range(nch)], axis=0)
    o = o * lax.rsqrt(_segsum(o * o, e) * (1.0 / DV) + EPS) * og_ref[...]
    z = z_ref[...].astype(F32)
    o_ref[...] = (o * (z * jax.nn.sigmoid(z))).astype(o_ref.dtype)


def _deltanet(proj, conv_w, alog, dtb, og, layer, batch, seq, tm=256):
    n = proj.shape[0]
    nt = seq // tm
    hw = H * DK
    e = jnp.asarray(np.kron(np.eye(H), np.ones((DK, DK))), BF16)
    spread = np.zeros((LANE, 2 * hw), np.float32)
    spread[:H, :hw] = np.kron(np.eye(H), np.ones((1, DK)))
    spread[H:2 * H, hw:] = np.kron(np.eye(H), np.ones((1, DK)))
    ex = jnp.asarray(spread, BF16)
    full = lambda shp: _layer_spec(shp, layer)
    const = lambda shp: pl.BlockSpec(shp, lambda b, i: (0, 0))
    blk = lambda unit, w: pl.BlockSpec((tm, w), lambda b, i: (b * nt + i, unit * LANE // w))
    return pl.pallas_call(
        functools.partial(_deltanet_kernel, tm=tm),
        grid=(batch, nt),
        in_specs=[blk(U_QKV, 3 * hw), blk(U_CZ, hw), blk(U_BA, LANE),
                  full((CONV, 3 * hw)), full((1, LANE)), full((1, LANE)), full((1, hw)),
                  const((hw, hw)), const((LANE, 2 * hw))],
        out_specs=pl.BlockSpec((tm, hw), lambda b, i: (b * nt + i, 0)),
        out_shape=jax.ShapeDtypeStruct((n, hw), BF16),
        scratch_shapes=[pltpu.VMEM((8, 3 * hw), F32),
                        pltpu.VMEM((hw // GW, GW, GW), F32)],
        compiler_params=_cparams(("parallel", "arbitrary")),
        name="deltanet",
    )(proj, proj, proj, conv_w, alog, dtb, og, e, ex)


def _mix_ffn_kernel(x_ref, ya_ref, yb_ref, yc_ref, g0_ref, g1_ref, g2_ref, wb_ref, wo_ref, gn_ref,
                    w1_ref, w2_ref, o_ref, a_ref, *, tf):
    merged = None
    for i, (y_ref, g_ref) in enumerate(((ya_ref, g0_ref), (yb_ref, g1_ref), (yc_ref, g2_ref))):
        y = jnp.dot(y_ref[...], wb_ref[i], preferred_element_type=F32)
        term = jax.nn.sigmoid(g_ref[...].astype(F32)) * y
        merged = term if merged is None else merged + term
    x1 = x_ref[...] + jnp.dot(merged.astype(BF16), wo_ref[...], preferred_element_type=F32)
    h = _rms(x1, gn_ref[...]).astype(BF16)
    for f in range(DFF // tf):
        cols = slice(f * tf, (f + 1) * tf)
        a = jnp.dot(h, w1_ref[:, cols], preferred_element_type=F32)
        a_ref[:, cols] = jnp.square(jnp.maximum(a, 0.0)).astype(BF16)
    o_ref[...] = x1 + jnp.dot(a_ref[...], w2_ref[...], preferred_element_type=F32)


def _mix_ffn(x, ya, yb, yc, proj, wb, wo, gn, w1, w2, layer, tm=512, tf=1024):
    n = x.shape[0]
    tok = lambda w: pl.BlockSpec((tm, w), lambda r: (r, 0))
    gate = lambda i: pl.BlockSpec((tm, D), lambda r, i=i: (r, U_GATE // 8 + i))
    res = lambda shp: _layer_spec(shp, layer, single_buffer=True)
    return pl.pallas_call(
        functools.partial(_mix_ffn_kernel, tf=tf),
        grid=(n // tm,),
        in_specs=[tok(D), tok(BW), tok(BW), tok(BW), gate(0), gate(1), gate(2),
                  res((NB, BW, D)), res((D, D)), res((1, D)), res((D, DFF)), res((DFF, D))],
        out_specs=tok(D),
        out_shape=jax.ShapeDtypeStruct((n, D), F32),
        scratch_shapes=[pltpu.VMEM((tm, DFF), BF16)],
        compiler_params=_cparams(("parallel",)),
        name="mix_ffn",
    )(x, ya, yb, yc, proj, proj, proj, wb, wo, gn, w1, w2)


def _pad_heads(w, per_head, first, used):
    depth, kdim, _ = w.shape
    w = w.reshape(depth, kdim, H, per_head)[..., first:first + used]
    w = jnp.pad(w, ((0, 0), (0, 0), (0, 0), (0, LANE - used)))
    return w.reshape(depth, kdim, H * LANE).astype(BF16)


def kernel(x, positions, norm1_g, w_in, sgu_norm_g, w_spatial, b_spatial, q_lat_norm_g, w_q_up,
           kv_lat_norm_g, w_kv_up, q_norm_g, k_norm_g, conv_w, a_log, dt_bias, o_norm_g,
           w_branch, w_out, norm2_g, w_ff1, w_ff2):
    batch, seq, d = x.shape
    depth = w_in.shape[0]
    n = batch * seq
    xt = x.reshape(n, d)
    ctab, stab = _rope_tables(positions.reshape(n, 1))
    rows = lambda a: a.reshape(depth, 1, -1).astype(F32)
    half = ROPE // 2
    w_perm = _permute_w_in(w_in)
    wcat = (w_spatial.reshape(depth, A_G // 2, 2, A_T, A_T).transpose(0, 1, 3, 2, 4)
            .reshape(depth, A_G // 2, A_T, 2 * A_T).astype(F32))
    sgu_bias = jnp.repeat(jnp.swapaxes(b_spatial, 1, 2), A_W // A_G, axis=2).astype(F32)
    wq = _pad_heads(w_q_up, QK, 0, QK)
    wk = _pad_heads(w_kv_up, NOPE + VD, 0, NOPE)
    wv = _pad_heads(w_kv_up, NOPE + VD, NOPE, VD)
    pad_g = lambda g: jnp.pad(
        jnp.stack([g, jnp.concatenate([g[:, :NOPE], g[:, NOPE + half:], g[:, NOPE:NOPE + half]], axis=1)],
                  axis=1).astype(F32), ((0, 0), (0, 0), (0, LANE - QK)))
    qg, kg = pad_g(q_norm_g), pad_g(k_norm_g)
    at_ca = lambda a: jnp.pad(a.astype(F32), ((0, 0), (H, LANE - 2 * H))).reshape(depth, 1, LANE)
    alog, dtb = at_ca(a_log), at_ca(dt_bias)
    og = jnp.tile(o_norm_g.astype(F32), (1, H)).reshape(depth, 1, H * DV)
    conv_f = conv_w.astype(F32)
    wb, wo, w1, w2 = (w.astype(BF16) for w in (w_branch, w_out, w_ff1, w_ff2))
    g1, g2, gs, gq, gkv = (rows(g) for g in (norm1_g, norm2_g, sgu_norm_g, q_lat_norm_g, kv_lat_norm_g))

    for l in range(depth):
        proj, q, k, v, y_a = _inproj_mla(xt, g1, w_perm, ctab, stab, gq, gkv, wq, wk, wv, qg, kg,
                                         gs, wcat, sgu_bias, l)
        y_b = _attention(q, k, v, batch, seq)
        y_c = _deltanet(proj, conv_f, alog, dtb, og, l, batch, seq)
        xt = _mix_ffn(xt, y_a, y_b, y_c, proj, wb, wo, g2, w1, w2, l)
    return xt.reshape(batch, seq, d)
```

```python
import functools
import math

import numpy as np
import jax
import jax.numpy as jnp
from jax import lax
from jax.experimental import pallas as pl
from jax.experimental.pallas import tpu as pltpu

F32 = jnp.float32
BF16 = jnp.bfloat16

LANE = 128
VMEM_LIMIT = 56 * 1024 * 1024

D = 1024
A_W = 512
A_G = 8
A_T = 128
H = 8
NOPE = 64
ROPE = 32
VD = 64
QK = NOPE + ROPE
Q_LORA = 384
KV_LORA = 256
ROPE_BASE = 10000.0
DK = 64
DV = 64
CONV = 4
CC = 64
SB = 16
NB = 3
BW = 512
DFF = 4 * D
EPS = 1e-6

U_MLA, U_BA, U_CZ, U_QKV, U_AU, U_AV, U_GATE = 0, 6, 8, 12, 24, 28, 32
DP = 56 * LANE

O_AU, O_AV, O_QLAT, O_KVLAT, O_KROPE = 0, 512, 1024, 1408, 1664
O_CQKV, O_CZ, O_CB, O_CA, O_GATES = 1696, 3232, 3744, 3752, 3760


def _permute_w_in(w):
    w = w.astype(BF16)
    sl = lambda off, n: w[..., off:off + n]
    zeros = lambda n: jnp.zeros(w.shape[:-1] + (n,), w.dtype)
    pieces = [
        sl(O_QLAT, Q_LORA), sl(O_KVLAT, KV_LORA),
        zeros(NOPE), sl(O_KROPE, ROPE), zeros(LANE - QK),
        sl(O_CB, H), sl(O_CA, H), zeros((U_CZ - U_BA) * LANE - 2 * H),
        sl(O_CZ, 512), sl(O_CQKV, 1536), sl(O_AU, A_W), sl(O_AV, A_W),
        sl(O_GATES, NB * D),
    ]
    out = jnp.concatenate(pieces, axis=-1).astype(BF16)
    assert out.shape[-1] == DP
    return out


def _cparams(sem):
    return pltpu.CompilerParams(dimension_semantics=sem, vmem_limit_bytes=VMEM_LIMIT)


def _layer_spec(shape, layer, single_buffer=False):
    index_map = lambda *_: (layer,) + (0,) * len(shape)
    if single_buffer:
        return pl.BlockSpec((None,) + shape, index_map, pipeline_mode=pl.Buffered(1))
    return pl.BlockSpec((None,) + shape, index_map)


def _rms(x, g):
    ms = jnp.mean(x * x, axis=-1, keepdims=True)
    return x * lax.rsqrt(ms + EPS) * g


def _inproj_mla_kernel(x_ref, g_ref, w_ref, c_ref, s_ref, gq_ref, gkv_ref, wq_ref, wk_ref, wv_ref,
                       qg_ref, kg_ref, gs_ref, ws_ref, bs_ref, o_ref, q_ref, k_ref, v_ref, ya_ref, *, tn):
    h = _rms(x_ref[...], g_ref[...]).astype(BF16)
    assert (U_AU * LANE) % tn == 0 and U_AV * LANE == U_AU * LANE + A_W and tn == 2 * A_W
    sgu_chunk = U_AU * LANE // tn
    pending = [0, sgu_chunk] + [j for j in range(w_ref.shape[1] // tn) if j not in (0, sgu_chunk)]

    def project_chunk():
        if not pending:
            return None
        j = pending.pop(0)
        cols = slice(j * tn, (j + 1) * tn)
        out = jnp.dot(h, w_ref[:, cols], preferred_element_type=F32).astype(o_ref.dtype)
        o_ref[:, cols] = out
        return out

    p = project_chunk()[:, :Q_LORA + KV_LORA + LANE].astype(F32)
    ql = p[:, :Q_LORA]
    kvl = p[:, Q_LORA:Q_LORA + KV_LORA]
    kr = p[:, Q_LORA + KV_LORA:]
    qn = _rms(ql, gq_ref[...]).astype(BF16)
    kvn = _rms(kvl, gkv_ref[...]).astype(BF16)
    q = jnp.dot(qn, wq_ref[...], preferred_element_type=F32)
    kn = jnp.dot(kvn, wk_ref[...], preferred_element_type=F32)
    v = jnp.dot(kvn, wv_ref[...], preferred_element_type=F32)
    vlane = lax.broadcasted_iota(jnp.int32, v.shape, 1) % LANE
    v_ref[...] = jnp.where(vlane == VD, 1.0, v).astype(v_ref.dtype)
    uv = project_chunk()
    _sgu_body(uv[:, :A_W], uv[:, A_W:], gs_ref, ws_ref, bs_ref, ya_ref)
    cs = c_ref[...]
    sn = s_ref[...]
    lane = lax.broadcasted_iota(jnp.int32, cs.shape, 1)
    upper = lane >= NOPE + ROPE // 2
    scale = QK ** -0.5 * math.log2(math.e)

    q_tabs = (cs * qg_ref[0:1, :], sn * qg_ref[1:2, :])
    k_tabs = (cs * kg_ref[0:1, :], sn * kg_ref[1:2, :])

    tiles = []
    for hd in range(H):
        cols = slice(hd * LANE, (hd + 1) * LANE)
        tiles.append((q_ref, cols, q[:, cols], q_tabs, scale))
        tiles.append((k_ref, cols, kn[:, cols] + kr, k_tabs, 1.0))
    sums = [jnp.sum(x * x, axis=-1, keepdims=True) for _, _, x, _, _ in tiles]
    project_chunk()
    rs = [lax.rsqrt(ss * (1.0 / QK) + EPS) * sc for ss, (_, _, _, _, sc) in zip(sums, tiles)]
    rot = []
    for i, (_, _, x, tabs, _) in enumerate(tiles):
        if i % 4 == 0 and i:
            project_chunk()
        partner = jnp.where(upper, pltpu.roll(x, ROPE // 2, 1), pltpu.roll(x, LANE - ROPE // 2, 1))
        rot.append(x * tabs[0] + partner * tabs[1])
    while pending:
        project_chunk()
    for (ref, cols, _, _, _), y, r in zip(tiles, rot, rs):
        ref[:, cols] = (y * r).astype(ref.dtype)


def _inproj_mla(x, g, w, ctab, stab, gq, gkv, wq, wk, wv, qg, kg, gs, wcat, sgu_bias, layer, tm=512, tn=1024):
    n = x.shape[0]
    dp = w.shape[-1]
    full = lambda shp: _layer_spec(shp, layer)
    tok = lambda width: pl.BlockSpec((tm, width), lambda i: (i, 0))
    return pl.pallas_call(
        functools.partial(_inproj_mla_kernel, tn=tn),
        grid=(n // tm,),
        in_specs=[tok(D), full((1, D)), _layer_spec((D, dp), layer, single_buffer=True),
                  tok(LANE), tok(LANE),
                  full((1, Q_LORA)), full((1, KV_LORA)),
                  full((Q_LORA, H * LANE)), full((KV_LORA, H * LANE)), full((KV_LORA, H * LANE)),
                  full((2, LANE)), full((2, LANE)),
                  full((1, A_W)), full((A_G // 2, A_T, 2 * A_T)), full((A_T, A_W))],
        out_specs=[tok(dp), tok(H * LANE), tok(H * LANE), tok(H * LANE), tok(A_W)],
        out_shape=[jax.ShapeDtypeStruct((n, dp), BF16)] + [jax.ShapeDtypeStruct((n, H * LANE), BF16)] * 3
        + [jax.ShapeDtypeStruct((n, A_W), BF16)],
        compiler_params=_cparams(("parallel",)),
        name="inproj_mla",
    )(x, g, w, ctab, stab, gq, gkv, wq, wk, wv, qg, kg, gs, wcat, sgu_bias)


def _rope_kernel(pos_ref, invf_ref, c_ref, s_ref):
    pos = pos_ref[...].astype(F32)
    ang = pos * invf_ref[...]
    lane = lax.broadcasted_iota(jnp.int32, ang.shape, 1)
    c = jnp.cos(ang)
    s = jnp.sin(ang)
    half = ROPE // 2
    c_ref[...] = jnp.where((lane >= NOPE) & (lane < QK), c, 1.0)
    s_ref[...] = jnp.where(lane < NOPE, 0.0,
                           jnp.where(lane < NOPE + half, -s, jnp.where(lane < QK, s, 0.0)))


def _rope_tables(pos, tm=1024):
    n = pos.shape[0]
    half = ROPE // 2
    lane = np.arange(LANE)
    invf = np.where((lane >= NOPE) & (lane < QK),
                    1.0 / (ROPE_BASE ** (((lane - NOPE) % half).astype(np.float32) / half)), 0.0)
    invf = jnp.asarray(invf, F32).reshape(1, LANE)
    return pl.pallas_call(
        _rope_kernel,
        grid=(n // tm,),
        in_specs=[pl.BlockSpec((tm, 1), lambda i: (i, 0)),
                  pl.BlockSpec((1, LANE), lambda i: (0, 0))],
        out_specs=[pl.BlockSpec((tm, LANE), lambda i: (i, 0)),
                   pl.BlockSpec((tm, LANE), lambda i: (i, 0))],
        out_shape=[jax.ShapeDtypeStruct((n, LANE), F32)] * 2,
        compiler_params=_cparams(("parallel",)),
        name="rope_tables",
    )(pos, invf)


def _gelu(x):
    return 0.5 * x * (1.0 + lax.erf(x * (1.0 / math.sqrt(2.0))))


def _sgu_body(a_u, a_v, g_ref, w_ref, b_ref, o_ref):
    tm = a_u.shape[0]
    u = _gelu(a_u.astype(F32))
    v = _gelu(a_v.astype(F32))
    ms = jnp.mean(v * v, axis=-1, keepdims=True)
    v = v * lax.rsqrt(ms + EPS) * g_ref[...]
    lo = (lax.broadcasted_iota(jnp.int32, v.shape, 1) % LANE) < (A_W // A_G)
    v_lo = jnp.where(lo, v, 0.0).astype(BF16)
    v_hi = jnp.where(lo, 0.0, v).astype(BF16)
    r = lax.broadcasted_iota(jnp.int32, (A_T, 2 * A_T), 0)
    c = lax.broadcasted_iota(jnp.int32, (A_T, 2 * A_T), 1)
    causal = (c % A_T) <= r
    for p in range(A_G // 2):
        cols = slice(p * LANE, (p + 1) * LANE)
        w = jnp.where(causal, w_ref[p], 0.0).astype(BF16)
        for ch in range(tm // A_T):
            rows = slice(ch * A_T, (ch + 1) * A_T)
            rhs = jnp.concatenate([v_lo[rows, cols], v_hi[rows, cols]], axis=0)
            sv = jnp.dot(w, rhs, preferred_element_type=F32) + b_ref[:, cols]
            o_ref[rows, cols] = (u[rows, cols] * sv).astype(o_ref.dtype)


def _attn_kernel(q_ref, k_ref, v_ref, o_ref, *, t):
    qi = pl.program_id(2)
    heads = (slice(0, LANE), slice(LANE, 2 * LANE))
    qs = [q_ref[:, c] for c in heads]

    def step(start, w, carry, masked):
        ss = [lax.dot_general(q, k_ref[pl.ds(start, w), c], (((1,), (1,)), ((), ())),
                              preferred_element_type=F32) for c, q in zip(heads, qs)]
        if masked:
            row = lax.broadcasted_iota(jnp.int32, (t, w), 0)
            col = lax.broadcasted_iota(jnp.int32, (t, w), 1)
            ok = col <= row + (w - t)
            ss = [jnp.where(ok, s, -jnp.inf) for s in ss]
        ms = [jnp.maximum(m, jnp.max(s, axis=-1, keepdims=True)) for s, (m, _) in zip(ss, carry)]
        ps = [jnp.exp2(s - m_new).astype(BF16) for s, m_new in zip(ss, ms)]
        pv = [jnp.dot(p, v_ref[pl.ds(start, w), c], preferred_element_type=F32) for p, c in zip(ps, heads)]
        return tuple((m_new, jnp.exp2(m - m_new) * acc + x) for m_new, (m, acc), x in zip(ms, carry, pv))

    init = tuple((jnp.full((t, 1), -jnp.inf, F32), jnp.zeros((t, LANE), F32)) for _ in heads)
    carry = lax.fori_loop(
        0, qi // 2, lambda j, c: step(pl.multiple_of(j * (2 * t), 2 * t), 2 * t, c, False), init)
    (_, acc0), (_, acc1) = lax.cond(
        qi % 2 == 1,
        lambda c: step(pl.multiple_of((qi - 1) * t, 2 * t), 2 * t, c, True),
        lambda c: step(pl.multiple_of(qi * t, t), t, c, True),
        carry)
    out0 = acc0 / acc0[:, VD:VD + 1]
    out1 = acc1 / acc1[:, VD:VD + 1]
    lane = lax.broadcasted_iota(jnp.int32, (t, LANE), 1)
    o_ref[...] = jnp.where(lane < VD, out0, pltpu.roll(out1, VD, 1)).astype(o_ref.dtype)


def _attention(q, k, v, batch, seq, t=1024):
    n = q.shape[0]
    nq = seq // t
    k3 = k.reshape(batch, seq, H * LANE)
    v3 = v.reshape(batch, seq, H * LANE)
    return pl.pallas_call(
        functools.partial(_attn_kernel, t=t),
        grid=(batch, H // 2, nq),
        in_specs=[pl.BlockSpec((t, 2 * LANE), lambda b, hp, i: (b * nq + i, hp)),
                  pl.BlockSpec((None, seq, 2 * LANE), lambda b, hp, i: (b, 0, hp)),
                  pl.BlockSpec((None, seq, 2 * LANE), lambda b, hp, i: (b, 0, hp))],
        out_specs=pl.BlockSpec((t, 2 * VD), lambda b, hp, i: (b * nq + i, hp)),
        out_shape=jax.ShapeDtypeStruct((n, H * VD), BF16),
        compiler_params=_cparams(("parallel", "parallel", "arbitrary")),
        name="attention",
    )(q, k3, v3)


GW = 4 * DK


def _blockdiag(x, mask):
    return jnp.where(mask, jnp.concatenate([x] * (GW // CC), axis=0), jnp.zeros((), x.dtype))


def _split3(x):
    hi = x.astype(BF16)
    r = x - hi.astype(F32)
    mid = r.astype(BF16)
    lo = (r - mid.astype(F32)).astype(BF16)
    return hi, mid, lo


def _dot_exact_rhs(sel, x):
    n = x.shape[1]
    r = jnp.dot(sel, jnp.concatenate(_split3(x), axis=1), preferred_element_type=F32)
    return r[:, :n] + r[:, n:2 * n] + r[:, 2 * n:]


def _bd_dot(a, b, mask):
    return jnp.dot(a.astype(BF16), _blockdiag(b.astype(BF16), mask), preferred_element_type=F32)


def _segsum(x, e):
    return jnp.dot(x.astype(BF16), e, preferred_element_type=F32)


def _deltanet_kernel(qkv_ref, z_ref, ba_ref, cw_ref, alog_ref, dtb_ref, og_ref, e_ref, ex_ref,
                     o_ref, tail_ref, state_ref, *, tm):
    t_idx = pl.program_id(1)
    nch = tm // CC
    ng = (H * DK) // GW
    width = 3 * H * DK

    @pl.when(t_idx == 0)
    def _():
        tail_ref[...] = jnp.zeros(tail_ref.shape, F32)
        state_ref[...] = jnp.zeros(state_ref.shape, F32)

    xb = qkv_ref[...]
    x = xb.astype(F32)
    sr = lax.broadcasted_iota(jnp.int32, ((CONV - 1) * tm, tm), 0)
    sc = lax.broadcasted_iota(jnp.int32, ((CONV - 1) * tm, tm), 1)
    sel = jnp.where(sc == (sr % tm) - (CONV - 1 - sr // tm), 1.0, 0.0).astype(BF16)
    shifted = jnp.dot(sel, xb, preferred_element_type=F32)
    xc = cw_ref[CONV - 1:CONV, :] * x
    for j in range(CONV - 1):
        xc = xc + cw_ref[j:j + 1, :] * shifted[j * tm:(j + 1) * tm]
    head = jnp.concatenate([tail_ref[...], x[0:8]], axis=0)
    fix = cw_ref[CONV - 1:CONV, :] * x[0:8]
    for j in range(CONV - 1):
        fix = fix + cw_ref[j:j + 1, :] * head[8 - (CONV - 1) + j: 16 - (CONV - 1) + j]
    xc = jnp.concatenate([fix, xc[8:]], axis=0)
    tail_ref[...] = x[tm - 8:tm]
    xc = xc * jax.nn.sigmoid(xc)
    e = e_ref[...]
    hw = H * DK
    q = xc[:, :hw]
    k = xc[:, hw:2 * hw]
    v = xc[:, 2 * hw:]
    q = q * lax.rsqrt(_segsum(q * q, e) + EPS) * (DK ** -0.5)
    k = k * lax.rsqrt(_segsum(k * k, e) + EPS)
    ba = ba_ref[...].astype(F32)
    sp_in = ba + dtb_ref[...]
    softplus = jnp.maximum(sp_in, 0.0) + jnp.log1p(jnp.exp(-jnp.abs(sp_in)))
    is_b = lax.broadcasted_iota(jnp.int32, ba.shape, 1) < H
    gates = jnp.where(is_b, jax.nn.sigmoid(ba), -jnp.exp(alog_ref[...]) * softplus)
    pieces = jnp.dot(jnp.concatenate(_split3(gates), axis=0), ex_ref[...], preferred_element_type=F32)
    spread = pieces[:tm] + pieces[tm:2 * tm] + pieces[2 * tm:]
    beta = spread[:, :hw]
    g = spread[:, hw:]

    ri = lax.broadcasted_iota(jnp.int32, (tm, tm), 0)
    ci = lax.broadcasted_iota(jnp.int32, (tm, tm), 1)
    cum = jnp.where((ri // CC == ci // CC) & (ci <= ri), 1.0, 0.0).astype(BF16)
    gc = _dot_exact_rhs(cum, g)
    egc = jnp.exp(gc)
    kb = k * beta
    vb = v * beta
    kbg = kb * egc
    qg = q * egc

    r64 = lax.broadcasted_iota(jnp.int32, (CC, GW), 0)
    l64 = lax.broadcasted_iota(jnp.int32, (CC, GW), 1) % CC
    tril = l64 <= r64
    strict = l64 < r64
    eye_t = jnp.where(l64 == r64, 1.0, 0.0).astype(F32)
    rb = lax.broadcasted_iota(jnp.int32, (GW, GW), 0) // CC
    lb = lax.broadcasted_iota(jnp.int32, (GW, GW), 1) // CC
    bd = rb == lb
    ones = jnp.ones((CC, CC), BF16)

    tiles = [(slice(ch * CC, (ch + 1) * CC), slice(gi * GW, (gi + 1) * GW), ch, gi)
             for ch in range(nch) for gi in range(ng)]
    low_l, intra_l = [], []
    for rows, cols, _, _ in tiles:
        gc_c = gc[rows, cols]
        bdk = _blockdiag(k[rows, cols].astype(BF16), bd)
        lhs = jnp.concatenate([kb[rows, cols], q[rows, cols]], axis=0).astype(BF16)
        a2 = lax.dot_general(lhs, bdk, (((1,), (1,)), ((), ())), preferred_element_type=F32)
        gct = _dot_exact_rhs(ones, jnp.where(l64 == r64, gc_c, 0.0))
        decay = jnp.exp(jnp.where(tril, gc_c - gct, -jnp.inf))
        low_l.append(jnp.where(strict, a2[:CC] * decay, 0.0))
        intra_l.append(jnp.where(tril, a2[CC:] * decay, 0.0).astype(BF16))
    n_tiles = len(tiles)
    diag_l = [jnp.where(r64 // SB == l64 // SB, low, 0.0) for low in low_l]
    p_l = [eye_t - d for d in diag_l]
    m_l = [_bd_dot(d, d, bd) for d in diag_l]
    n_sq = int(math.log2(SB)) - 1
    for lvl in range(n_sq):
        for i in range(n_tiles):
            if lvl < n_sq - 1:
                r = _bd_dot(jnp.concatenate([p_l[i], m_l[i]], axis=0), m_l[i], bd)
                p_l[i] = p_l[i] + r[:CC]
                m_l[i] = r[CC:]
            else:
                p_l[i] = p_l[i] + _bd_dot(p_l[i], m_l[i], bd)
    size = SB
    while size < CC:
        off = (r64 // (2 * size) == l64 // (2 * size)) & (r64 // size != l64 // size)
        x_l = [_bd_dot(p, jnp.where(off, low, 0.0), bd) for p, low in zip(p_l, low_l)]
        p_l = [p - _bd_dot(x, p, bd) for p, x in zip(p_l, x_l)]
        size *= 2
    uw_l = []
    for (rows, cols, _, _), p_acc in zip(tiles, p_l):
        rhs = jnp.concatenate([_blockdiag(vb[rows, cols].astype(BF16), bd),
                               _blockdiag(kbg[rows, cols].astype(BF16), bd)], axis=1)
        uw_l.append(jnp.dot(p_acc.astype(BF16), rhs, preferred_element_type=F32))

    states = [state_ref[gi] for gi in range(ng)]
    o_l = []
    for (rows, cols, ch, gi), uw, intra in zip(tiles, uw_l, intra_l):
        st = states[gi]
        lhs2 = jnp.concatenate([uw[:, GW:], qg[rows, cols]], axis=0).astype(BF16)
        ws = jnp.dot(lhs2, st.astype(BF16), preferred_element_type=F32)
        v_new = (uw[:, :GW] - ws[:CC]).astype(BF16)
        o = ws[CC:] + jnp.dot(intra, _blockdiag(v_new, bd), preferred_element_type=F32)
        g_last = gc[ch * CC + CC - 1: ch * CC + CC, cols]
        kd = (k[rows, cols] * jnp.exp(g_last - gc[rows, cols])).astype(BF16)
        upd = lax.dot_general(kd, v_new, (((0,), (0,)), ((), ())), preferred_element_type=F32)
        states[gi] = st * jnp.exp(g_last) + jnp.where(bd, upd, 0.0)
        o_l.append(o)
    for gi in range(ng):
        state_ref[gi] = states[gi]

    o = jnp.concatenate([jnp.concatenate(o_l[ch * ng:(ch + 1) * ng], axis=1) for ch in range(nch)], axis=0)
    o = o * lax.rsqrt(_segsum(o * o, e) * (1.0 / DV) + EPS) * og_ref[...]
    z = z_ref[...].astype(F32)
    o_ref[...] = (o * (z * jax.nn.sigmoid(z))).astype(o_ref.dtype)


def _deltanet(proj, conv_w, alog, dtb, og, layer, batch, seq, tm=256):
    n = proj.shape[0]
    nt = seq // tm
    hw = H * DK
    e = jnp.asarray(np.kron(np.eye(H), np.ones((DK, DK))), BF16)
    spread = np.zeros((LANE, 2 * hw), np.float32)
    spread[:H, :hw] = np.kron(np.eye(H), np.ones((1, DK)))
    spread[H:2 * H, hw:] = np.kron(np.eye(H), np.ones((1, DK)))
    ex = jnp.asarray(spread, BF16)
    full = lambda shp: _layer_spec(shp, layer)
    const = lambda shp: pl.BlockSpec(shp, lambda b, i: (0, 0))
    blk = lambda unit, w: pl.BlockSpec((tm, w), lambda b, i: (b * nt + i, unit * LANE // w))
    return pl.pallas_call(
        functools.partial(_deltanet_kernel, tm=tm),
        grid=(batch, nt),
        in_specs=[blk(U_QKV, 3 * hw), blk(U_CZ, hw), blk(U_BA, LANE),
                  full((CONV, 3 * hw)), full((1, LANE)), full((1, LANE)), full((1, hw)),
                  const((hw, hw)), const((LANE, 2 * hw))],
        out_specs=pl.BlockSpec((tm, hw), lambda b, i: (b * nt + i, 0)),
        out_shape=jax.ShapeDtypeStruct((n, hw), BF16),
        scratch_shapes=[pltpu.VMEM((8, 3 * hw), F32),
                        pltpu.VMEM((hw // GW, GW, GW), F32)],
        compiler_params=_cparams(("parallel", "arbitrary")),
        name="deltanet",
    )(proj, proj, proj, conv_w, alog, dtb, og, e, ex)


def _mix_ffn_kernel(x_ref, ya_ref, yb_ref, yc_ref, g0_ref, g1_ref, g2_ref, wb_ref, wo_ref, gn_ref,
                    w1_ref, w2_ref, o_ref, a_ref, *, tf):
    merged = None
    for i, (y_ref, g_ref) in enumerate(((ya_ref, g0_ref), (yb_ref, g1_ref), (yc_ref, g2_ref))):
        y = jnp.dot(y_ref[...], wb_ref[i], preferred_element_type=F32)
        term = jax.nn.sigmoid(g_ref[...].astype(F32)) * y
        merged = term if merged is None else merged + term
    x1 = x_ref[...] + jnp.dot(merged.astype(BF16), wo_ref[...], preferred_element_type=F32)
    h = _rms(x1, gn_ref[...]).astype(BF16)
    for f in range(DFF // tf):
        cols = slice(f * tf, (f + 1) * tf)
        a = jnp.dot(h, w1_ref[:, cols], preferred_element_type=F32)
        a_ref[:, cols] = jnp.square(jnp.maximum(a, 0.0)).astype(BF16)
    o_ref[...] = x1 + jnp.dot(a_ref[...], w2_ref[...], preferred_element_type=F32)


def _mix_ffn(x, ya, yb, yc, proj, wb, wo, gn, w1, w2, layer, tm=512, tf=1024):
    n = x.shape[0]
    tok = lambda w: pl.BlockSpec((tm, w), lambda r: (r, 0))
    gate = lambda i: pl.BlockSpec((tm, D), lambda r, i=i: (r, U_GATE // 8 + i))
    res = lambda shp: _layer_spec(shp, layer, single_buffer=True)
    return pl.pallas_call(
        functools.partial(_mix_ffn_kernel, tf=tf),
        grid=(n // tm,),
        in_specs=[tok(D), tok(BW), tok(BW), tok(BW), gate(0), gate(1), gate(2),
                  res((NB, BW, D)), res((D, D)), res((1, D)), res((D, DFF)), res((DFF, D))],
        out_specs=tok(D),
        out_shape=jax.ShapeDtypeStruct((n, D), F32),
        scratch_shapes=[pltpu.VMEM((tm, DFF), BF16)],
        compiler_params=_cparams(("parallel",)),
        name="mix_ffn",
    )(x, ya, yb, yc, proj, proj, proj, wb, wo, gn, w1, w2)


def _pad_heads(w, per_head, first, used):
    depth, kdim, _ = w.shape
    w = w.reshape(depth, kdim, H, per_head)[..., first:first + used]
    w = jnp.pad(w, ((0, 0), (0, 0), (0, 0), (0, LANE - used)))
    return w.reshape(depth, kdim, H * LANE).astype(BF16)


def kernel(x, positions, norm1_g, w_in, sgu_norm_g, w_spatial, b_spatial, q_lat_norm_g, w_q_up,
           kv_lat_norm_g, w_kv_up, q_norm_g, k_norm_g, conv_w, a_log, dt_bias, o_norm_g,
           w_branch, w_out, norm2_g, w_ff1, w_ff2):
    batch, seq, d = x.shape
    depth = w_in.shape[0]
    n = batch * seq
    xt = x.reshape(n, d)
    ctab, stab = _rope_tables(positions.reshape(n, 1))
    rows = lambda a: a.reshape(depth, 1, -1).astype(F32)
    half = ROPE // 2
    w_perm = _permute_w_in(w_in)
    wcat = (w_spatial.reshape(depth, A_G // 2, 2, A_T, A_T).transpose(0, 1, 3, 2, 4)
            .reshape(depth, A_G // 2, A_T, 2 * A_T).astype(F32))
    sgu_bias = jnp.repeat(jnp.swapaxes(b_spatial, 1, 2), A_W // A_G, axis=2).astype(F32)
    wq = _pad_heads(w_q_up, QK, 0, QK)
    wk = _pad_heads(w_kv_up, NOPE + VD, 0, NOPE)
    wv = _pad_heads(w_kv_up, NOPE + VD, NOPE, VD)
    pad_g = lambda g: jnp.pad(
        jnp.stack([g, jnp.concatenate([g[:, :NOPE], g[:, NOPE + half:], g[:, NOPE:NOPE + half]], axis=1)],
                  axis=1).astype(F32), ((0, 0), (0, 0), (0, LANE - QK)))
    qg, kg = pad_g(q_norm_g), pad_g(k_norm_g)
    at_ca = lambda a: jnp.pad(a.astype(F32), ((0, 0), (H, LANE - 2 * H))).reshape(depth, 1, LANE)
    alog, dtb = at_ca(a_log), at_ca(dt_bias)
    og = jnp.tile(o_norm_g.astype(F32), (1, H)).reshape(depth, 1, H * DV)
    conv_f = conv_w.astype(F32)
    wb, wo, w1, w2 = (w.astype(BF16) for w in (w_branch, w_out, w_ff1, w_ff2))
    g1, g2, gs, gq, gkv = (rows(g) for g in (norm1_g, norm2_g, sgu_norm_g, q_lat_norm_g, kv_lat_norm_g))

    for l in range(depth):
        proj, q, k, v, y_a = _inproj_mla(xt, g1, w_perm, ctab, stab, gq, gkv, wq, wk, wv, qg, kg,
                                         gs, wcat, sgu_bias, l)
        y_b = _attention(q, k, v, batch, seq)
        y_c = _deltanet(proj, conv_f, alog, dtb, og, l, batch, seq)
        xt = _mix_ffn(xt, y_a, y_b, y_c, proj, wb, wo, g2, w1, w2, l)
    return xt.reshape(batch, seq, d)
```

```python
import functools
import math

import numpy as np
import jax
import jax.numpy as jnp
from jax import lax
from jax.experimental import pallas as pl
from jax.experimental.pallas import tpu as pltpu

F32 = jnp.float32
BF16 = jnp.bfloat16

LANE = 128
VMEM_LIMIT = 56 * 1024 * 1024

D = 1024
A_W = 512
A_G = 8
A_T = 128
H = 8
NOPE = 64
ROPE = 32
VD = 64
QK = NOPE + ROPE
Q_LORA = 384
KV_LORA = 256
ROPE_BASE = 10000.0
DK = 64
DV = 64
CONV = 4
CC = 64
SB = 16
NB = 3
BW = 512
DFF = 4 * D
EPS = 1e-6

U_MLA, U_BA, U_CZ, U_QKV, U_AU, U_AV, U_GATE = 0, 6, 8, 12, 24, 28, 32
DP = 56 * LANE

O_AU, O_AV, O_QLAT, O_KVLAT, O_KROPE = 0, 512, 1024, 1408, 1664
O_CQKV, O_CZ, O_CB, O_CA, O_GATES = 1696, 3232, 3744, 3752, 3760


def _permute_w_in(w):
    w = w.astype(BF16)
    sl = lambda off, n: w[..., off:off + n]
    zeros = lambda n: jnp.zeros(w.shape[:-1] + (n,), w.dtype)
    pieces = [
        sl(O_QLAT, Q_LORA), sl(O_KVLAT, KV_LORA),
        zeros(NOPE), sl(O_KROPE, ROPE), zeros(LANE - QK),
        sl(O_CB, H), sl(O_CA, H), zeros((U_CZ - U_BA) * LANE - 2 * H),
        sl(O_CZ, 512), sl(O_CQKV, 1536), sl(O_AU, A_W), sl(O_AV, A_W),
        sl(O_GATES, NB * D),
    ]
    out = jnp.concatenate(pieces, axis=-1).astype(BF16)
    assert out.shape[-1] == DP
    return out


def _cparams(sem):
    return pltpu.CompilerParams(dimension_semantics=sem, vmem_limit_bytes=VMEM_LIMIT)


def _layer_spec(shape, layer, single_buffer=False):
    index_map = lambda *_: (layer,) + (0,) * len(shape)
    if single_buffer:
        return pl.BlockSpec((None,) + shape, index_map, pipeline_mode=pl.Buffered(1))
    return pl.BlockSpec((None,) + shape, index_map)


def _rms(x, g):
    ms = jnp.mean(x * x, axis=-1, keepdims=True)
    return x * lax.rsqrt(ms + EPS) * g


def _inproj_mla_kernel(x_ref, g_ref, w_ref, c_ref, s_ref, gq_ref, gkv_ref, wq_ref, wk_ref, wv_ref,
                       qg_ref, kg_ref, gs_ref, ws_ref, bs_ref, o_ref, q_ref, k_ref, v_ref, ya_ref, *, tn):
    h = _rms(x_ref[...], g_ref[...]).astype(BF16)
    assert (U_AU * LANE) % tn == 0 and U_AV * LANE == U_AU * LANE + A_W and tn == 2 * A_W
    sgu_chunk = U_AU * LANE // tn
    pending = [0, sgu_chunk] + [j for j in range(w_ref.shape[1] // tn) if j not in (0, sgu_chunk)]

    def project_chunk():
        if not pending:
            return None
        j = pending.pop(0)
        cols = slice(j * tn, (j + 1) * tn)
        out = jnp.dot(h, w_ref[:, cols], preferred_element_type=F32).astype(o_ref.dtype)
        o_ref[:, cols] = out
        return out

    first = project_chunk()
    uv = project_chunk()
    p = first[:, :Q_LORA + KV_LORA + LANE].astype(F32)
    ql = p[:, :Q_LORA]
    kvl = p[:, Q_LORA:Q_LORA + KV_LORA]
    kr = p[:, Q_LORA + KV_LORA:]
    qn = _rms(ql, gq_ref[...]).astype(BF16)
    kvn = _rms(kvl, gkv_ref[...]).astype(BF16)
    q = jnp.dot(qn, wq_ref[...], preferred_element_type=F32)
    kn = jnp.dot(kvn, wk_ref[...], preferred_element_type=F32)
    v = jnp.dot(kvn, wv_ref[...], preferred_element_type=F32)
    vlane = lax.broadcasted_iota(jnp.int32, v.shape, 1) % LANE
    v_ref[...] = jnp.where(vlane == VD, 1.0, v).astype(v_ref.dtype)
    u, v_lo, v_hi = _sgu_prepare(uv[:, :A_W], uv[:, A_W:], gs_ref)
    project_chunk()
    _sgu_mix(u, v_lo, v_hi, ws_ref, bs_ref, ya_ref)
    cs = c_ref[...]
    sn = s_ref[...]
    lane = lax.broadcasted_iota(jnp.int32, cs.shape, 1)
    upper = lane >= NOPE + ROPE // 2
    scale = QK ** -0.5 * math.log2(math.e)

    q_tabs = (cs * qg_ref[0:1, :], sn * qg_ref[1:2, :])
    k_tabs = (cs * kg_ref[0:1, :], sn * kg_ref[1:2, :])

    tiles = []
    for hd in range(H):
        cols = slice(hd * LANE, (hd + 1) * LANE)
        tiles.append((q_ref, cols, q[:, cols], q_tabs, scale))
        tiles.append((k_ref, cols, kn[:, cols] + kr, k_tabs, 1.0))
    sums = [jnp.sum(x * x, axis=-1, keepdims=True) for _, _, x, _, _ in tiles]
    project_chunk()
    rs = [lax.rsqrt(ss * (1.0 / QK) + EPS) * sc for ss, (_, _, _, _, sc) in zip(sums, tiles)]
    rot = []
    for i, (_, _, x, tabs, _) in enumerate(tiles):
        if i % 4 == 0 and i:
            project_chunk()
        partner = jnp.where(upper, pltpu.roll(x, ROPE // 2, 1), pltpu.roll(x, LANE - ROPE // 2, 1))
        rot.append(x * tabs[0] + partner * tabs[1])
    while pending:
        project_chunk()
    for (ref, cols, _, _, _), y, r in zip(tiles, rot, rs):
        ref[:, cols] = (y * r).astype(ref.dtype)


def _inproj_mla(x, g, w, ctab, stab, gq, gkv, wq, wk, wv, qg, kg, gs, wcat, sgu_bias, layer, tm=512, tn=1024):
    n = x.shape[0]
    dp = w.shape[-1]
    full = lambda shp: _layer_spec(shp, layer)
    tok = lambda width: pl.BlockSpec((tm, width), lambda i: (i, 0))
    return pl.pallas_call(
        functools.partial(_inproj_mla_kernel, tn=tn),
        grid=(n // tm,),
        in_specs=[tok(D), full((1, D)), _layer_spec((D, dp), layer, single_buffer=True),
                  tok(LANE), tok(LANE),
                  full((1, Q_LORA)), full((1, KV_LORA)),
                  full((Q_LORA, H * LANE)), full((KV_LORA, H * LANE)), full((KV_LORA, H * LANE)),
                  full((2, LANE)), full((2, LANE)),
                  full((1, A_W)), full((A_G // 2, A_T, 2 * A_T)), full((A_T, A_W))],
        out_specs=[tok(dp), tok(H * LANE), tok(H * LANE), tok(H * LANE), tok(A_W)],
        out_shape=[jax.ShapeDtypeStruct((n, dp), BF16)] + [jax.ShapeDtypeStruct((n, H * LANE), BF16)] * 3
        + [jax.ShapeDtypeStruct((n, A_W), BF16)],
        compiler_params=_cparams(("parallel",)),
        name="inproj_mla",
    )(x, g, w, ctab, stab, gq, gkv, wq, wk, wv, qg, kg, gs, wcat, sgu_bias)


def _rope_kernel(pos_ref, invf_ref, c_ref, s_ref):
    pos = pos_ref[...].astype(F32)
    ang = pos * invf_ref[...]
    lane = lax.broadcasted_iota(jnp.int32, ang.shape, 1)
    c = jnp.cos(ang)
    s = jnp.sin(ang)
    half = ROPE // 2
    c_ref[...] = jnp.where((lane >= NOPE) & (lane < QK), c, 1.0)
    s_ref[...] = jnp.where(lane < NOPE, 0.0,
                           jnp.where(lane < NOPE + half, -s, jnp.where(lane < QK, s, 0.0)))


def _rope_tables(pos, tm=1024):
    n = pos.shape[0]
    half = ROPE // 2
    lane = np.arange(LANE)
    invf = np.where((lane >= NOPE) & (lane < QK),
                    1.0 / (ROPE_BASE ** (((lane - NOPE) % half).astype(np.float32) / half)), 0.0)
    invf = jnp.asarray(invf, F32).reshape(1, LANE)
    return pl.pallas_call(
        _rope_kernel,
        grid=(n // tm,),
        in_specs=[pl.BlockSpec((tm, 1), lambda i: (i, 0)),
                  pl.BlockSpec((1, LANE), lambda i: (0, 0))],
        out_specs=[pl.BlockSpec((tm, LANE), lambda i: (i, 0)),
                   pl.BlockSpec((tm, LANE), lambda i: (i, 0))],
        out_shape=[jax.ShapeDtypeStruct((n, LANE), F32)] * 2,
        compiler_params=_cparams(("parallel",)),
        name="rope_tables",
    )(pos, invf)


def _gelu(x):
    return 0.5 * x * (1.0 + lax.erf(x * (1.0 / math.sqrt(2.0))))


def _sgu_prepare(a_u, a_v, g_ref):
    u = _gelu(a_u.astype(F32))
    v = _gelu(a_v.astype(F32))
    ms = jnp.mean(v * v, axis=-1, keepdims=True)
    v = v * lax.rsqrt(ms + EPS) * g_ref[...]
    lo = (lax.broadcasted_iota(jnp.int32, v.shape, 1) % LANE) < (A_W // A_G)
    return u, jnp.where(lo, v, 0.0).astype(BF16), jnp.where(lo, 0.0, v).astype(BF16)


def _sgu_mix(u, v_lo, v_hi, w_ref, b_ref, o_ref):
    tm = u.shape[0]
    r = lax.broadcasted_iota(jnp.int32, (A_T, 2 * A_T), 0)
    c = lax.broadcasted_iota(jnp.int32, (A_T, 2 * A_T), 1)
    causal = (c % A_T) <= r
    for p in range(A_G // 2):
        cols = slice(p * LANE, (p + 1) * LANE)
        w = jnp.where(causal, w_ref[p], 0.0).astype(BF16)
        for ch in range(tm // A_T):
            rows = slice(ch * A_T, (ch + 1) * A_T)
            rhs = jnp.concatenate([v_lo[rows, cols], v_hi[rows, cols]], axis=0)
            sv = jnp.dot(w, rhs, preferred_element_type=F32) + b_ref[:, cols]
            o_ref[rows, cols] = (u[rows, cols] * sv).astype(o_ref.dtype)


def _attn_kernel(q_ref, k_ref, v_ref, o_ref, *, t):
    qi = pl.program_id(2)
    heads = (slice(0, LANE), slice(LANE, 2 * LANE))
    qs = [q_ref[:, c] for c in heads]

    def step(start, w, carry, masked):
        ss = [lax.dot_general(q, k_ref[pl.ds(start, w), c], (((1,), (1,)), ((), ())),
                              preferred_element_type=F32) for c, q in zip(heads, qs)]
        if masked:
            row = lax.broadcasted_iota(jnp.int32, (t, w), 0)
            col = lax.broadcasted_iota(jnp.int32, (t, w), 1)
            ok = col <= row + (w - t)
            ss = [jnp.where(ok, s, -jnp.inf) for s in ss]
        ms = [jnp.maximum(m, jnp.max(s, axis=-1, keepdims=True)) for s, (m, _) in zip(ss, carry)]
        ps = [jnp.exp2(s - m_new).astype(BF16) for s, m_new in zip(ss, ms)]
        pv = [jnp.dot(p, v_ref[pl.ds(start, w), c], preferred_element_type=F32) for p, c in zip(ps, heads)]
        return tuple((m_new, jnp.exp2(m - m_new) * acc + x) for m_new, (m, acc), x in zip(ms, carry, pv))

    init = tuple((jnp.full((t, 1), -jnp.inf, F32), jnp.zeros((t, LANE), F32)) for _ in heads)
    carry = lax.fori_loop(
        0, qi // 2, lambda j, c: step(pl.multiple_of(j * (2 * t), 2 * t), 2 * t, c, False), init)
    (_, acc0), (_, acc1) = lax.cond(
        qi % 2 == 1,
        lambda c: step(pl.multiple_of((qi - 1) * t, 2 * t), 2 * t, c, True),
        lambda c: step(pl.multiple_of(qi * t, t), t, c, True),
        carry)
    out0 = acc0 / acc0[:, VD:VD + 1]
    out1 = acc1 / acc1[:, VD:VD + 1]
    lane = lax.broadcasted_iota(jnp.int32, (t, LANE), 1)
    o_ref[...] = jnp.where(lane < VD, out0, pltpu.roll(out1, VD, 1)).astype(o_ref.dtype)


def _attention(q, k, v, batch, seq, t=1024):
    n = q.shape[0]
    nq = seq // t
    k3 = k.reshape(batch, seq, H * LANE)
    v3 = v.reshape(batch, seq, H * LANE)
    return pl.pallas_call(
        functools.partial(_attn_kernel, t=t),
        grid=(batch, H // 2, nq),
        in_specs=[pl.BlockSpec((t, 2 * LANE), lambda b, hp, i: (b * nq + i, hp)),
                  pl.BlockSpec((None, seq, 2 * LANE), lambda b, hp, i: (b, 0, hp)),
                  pl.BlockSpec((None, seq, 2 * LANE), lambda b, hp, i: (b, 0, hp))],
        out_specs=pl.BlockSpec((t, 2 * VD), lambda b, hp, i: (b * nq + i, hp)),
        out_shape=jax.ShapeDtypeStruct((n, H * VD), BF16),
        compiler_params=_cparams(("parallel", "parallel", "arbitrary")),
        name="attention",
    )(q, k3, v3)


GW = 4 * DK


def _blockdiag(x, mask):
    return jnp.where(mask, jnp.concatenate([x] * (GW // CC), axis=0), jnp.zeros((), x.dtype))


def _split3(x):
    hi = x.astype(BF16)
    r = x - hi.astype(F32)
    mid = r.astype(BF16)
    lo = (r - mid.astype(F32)).astype(BF16)
    return hi, mid, lo


def _dot_exact_rhs(sel, x):
    n = x.shape[1]
    r = jnp.dot(sel, jnp.concatenate(_split3(x), axis=1), preferred_element_type=F32)
    return r[:, :n] + r[:, n:2 * n] + r[:, 2 * n:]


def _bd_dot(a, b, mask):
    return jnp.dot(a.astype(BF16), _blockdiag(b.astype(BF16), mask), preferred_element_type=F32)


def _segsum(x, e):
    return jnp.dot(x.astype(BF16), e, preferred_element_type=F32)


def _deltanet_kernel(qkv_ref, z_ref, ba_ref, cw_ref, alog_ref, dtb_ref, og_ref, e_ref, ex_ref,
                     o_ref, tail_ref, state_ref, *, tm):
    t_idx = pl.program_id(1)
    nch = tm // CC
    ng = (H * DK) // GW
    width = 3 * H * DK

    @pl.when(t_idx == 0)
    def _():
        tail_ref[...] = jnp.zeros(tail_ref.shape, F32)
        state_ref[...] = jnp.zeros(state_ref.shape, F32)

    xb = qkv_ref[...]
    x = xb.astype(F32)
    sr = lax.broadcasted_iota(jnp.int32, ((CONV - 1) * tm, tm), 0)
    sc = lax.broadcasted_iota(jnp.int32, ((CONV - 1) * tm, tm), 1)
    sel = jnp.where(sc == (sr % tm) - (CONV - 1 - sr // tm), 1.0, 0.0).astype(BF16)
    shifted = jnp.dot(sel, xb, preferred_element_type=F32)
    xc = cw_ref[CONV - 1:CONV, :] * x
    for j in range(CONV - 1):
        xc = xc + cw_ref[j:j + 1, :] * shifted[j * tm:(j + 1) * tm]
    head = jnp.concatenate([tail_ref[...], x[0:8]], axis=0)
    fix = cw_ref[CONV - 1:CONV, :] * x[0:8]
    for j in range(CONV - 1):
        fix = fix + cw_ref[j:j + 1, :] * head[8 - (CONV - 1) + j: 16 - (CONV - 1) + j]
    xc = jnp.concatenate([fix, xc[8:]], axis=0)
    tail_ref[...] = x[tm - 8:tm]
    xc = xc * jax.nn.sigmoid(xc)
    e = e_ref[...]
    hw = H * DK
    q = xc[:, :hw]
    k = xc[:, hw:2 * hw]
    v = xc[:, 2 * hw:]
    q = q * lax.rsqrt(_segsum(q * q, e) + EPS) * (DK ** -0.5)
    k = k * lax.rsqrt(_segsum(k * k, e) + EPS)
    ba = ba_ref[...].astype(F32)
    sp_in = ba + dtb_ref[...]
    softplus = jnp.maximum(sp_in, 0.0) + jnp.log1p(jnp.exp(-jnp.abs(sp_in)))
    is_b = lax.broadcasted_iota(jnp.int32, ba.shape, 1) < H
    gates = jnp.where(is_b, jax.nn.sigmoid(ba), -jnp.exp(alog_ref[...]) * softplus)
    pieces = jnp.dot(jnp.concatenate(_split3(gates), axis=0), ex_ref[...], preferred_element_type=F32)
    spread = pieces[:tm] + pieces[tm:2 * tm] + pieces[2 * tm:]
    beta = spread[:, :hw]
    g = spread[:, hw:]

    ri = lax.broadcasted_iota(jnp.int32, (tm, tm), 0)
    ci = lax.broadcasted_iota(jnp.int32, (tm, tm), 1)
    cum = jnp.where((ri // CC == ci // CC) & (ci <= ri), 1.0, 0.0).astype(BF16)
    gc = _dot_exact_rhs(cum, g)
    egc = jnp.exp(gc)
    kb = k * beta
    vb = v * beta
    kbg = kb * egc
    qg = q * egc

    r64 = lax.broadcasted_iota(jnp.int32, (CC, GW), 0)
    l64 = lax.broadcasted_iota(jnp.int32, (CC, GW), 1) % CC
    tril = l64 <= r64
    strict = l64 < r64
    eye_t = jnp.where(l64 == r64, 1.0, 0.0).astype(F32)
    rb = lax.broadcasted_iota(jnp.int32, (GW, GW), 0) // CC
    lb = lax.broadcasted_iota(jnp.int32, (GW, GW), 1) // CC
    bd = rb == lb
    ones = jnp.ones((CC, CC), BF16)

    tiles = [(slice(ch * CC, (ch + 1) * CC), slice(gi * GW, (gi + 1) * GW), ch, gi)
             for ch in range(nch) for gi in range(ng)]
    low_l, intra_l = [], []
    for rows, cols, _, _ in tiles:
        gc_c = gc[rows, cols]
        bdk = _blockdiag(k[rows, cols].astype(BF16), bd)
        lhs = jnp.concatenate([kb[rows, cols], q[rows, cols]], axis=0).astype(BF16)
        a2 = lax.dot_general(lhs, bdk, (((1,), (1,)), ((), ())), preferred_element_type=F32)
        gct = _dot_exact_rhs(ones, jnp.where(l64 == r64, gc_c, 0.0))
        decay = jnp.exp(jnp.where(tril, gc_c - gct, -jnp.inf))
        low_l.append(jnp.where(strict, a2[:CC] * decay, 0.0))
        intra_l.append(jnp.where(tril, a2[CC:] * decay, 0.0).astype(BF16))
    n_tiles = len(tiles)
    diag_l = [jnp.where(r64 // SB == l64 // SB, low, 0.0) for low in low_l]
    p_l = [eye_t - d for d in diag_l]
    m_l = [_bd_dot(d, d, bd) for d in diag_l]
    n_sq = int(math.log2(SB)) - 1
    for lvl in range(n_sq):
        for i in range(n_tiles):
            if lvl < n_sq - 1:
                r = _bd_dot(jnp.concatenate([p_l[i], m_l[i]], axis=0), m_l[i], bd)
                p_l[i] = p_l[i] + r[:CC]
                m_l[i] = r[CC:]
            else:
                p_l[i] = p_l[i] + _bd_dot(p_l[i], m_l[i], bd)
    size = SB
    while size < CC:
        off = (r64 // (2 * size) == l64 // (2 * size)) & (r64 // size != l64 // size)
        x_l = [_bd_dot(p, jnp.where(off, low, 0.0), bd) for p, low in zip(p_l, low_l)]
        p_l = [p - _bd_dot(x, p, bd) for p, x in zip(p_l, x_l)]
        size *= 2
    uw_l = []
    for (rows, cols, _, _), p_acc in zip(tiles, p_l):
        rhs = jnp.concatenate([_blockdiag(vb[rows, cols].astype(BF16), bd),
                               _blockdiag(kbg[rows, cols].astype(BF16), bd)], axis=1)
        uw_l.append(jnp.dot(p_acc.astype(BF16), rhs, preferred_element_type=F32))

    states = [state_ref[gi] for gi in range(ng)]
    o_l = []
    for (rows, cols, ch, gi), uw, intra in zip(tiles, uw_l, intra_l):
        st = states[gi]
        lhs2 = jnp.concatenate([uw[:, GW:], qg[rows, cols]], axis=0).astype(BF16)
        ws = jnp.dot(lhs2, st.astype(BF16), preferred_element_type=F32)
        v_new = (uw[:, :GW] - ws[:CC]).astype(BF16)
        o = ws[CC:] + jnp.dot(intra, _blockdiag(v_new, bd), preferred_element_type=F32)
        g_last = gc[ch * CC + CC - 1: ch * CC + CC, cols]
        kd = (k[rows, cols] * jnp.exp(g_last - gc[rows, cols])).astype(BF16)
        upd = lax.dot_general(kd, v_new, (((0,), (0,)), ((), ())), preferred_element_type=F32)
        states[gi] = st * jnp.exp(g_last) + jnp.where(bd, upd, 0.0)
        o_l.append(o)
    for gi in range(ng):
        state_ref[gi] = states[gi]

    o = jnp.concatenate([jnp.concatenate(o_l[ch * ng:(ch + 1) * ng], axis=1) for ch in range(nch)], axis=0)
    o = o * lax.rsqrt(_segsum(o * o, e) * (1.0 / DV) + EPS) * og_ref[...]
    z = z_ref[...].astype(F32)
    o_ref[...] = (o * (z * jax.nn.sigmoid(z))).astype(o_ref.dtype)


def _deltanet(proj, conv_w, alog, dtb, og, layer, batch, seq, tm=256):
    n = proj.shape[0]
    nt = seq // tm
    hw = H * DK
    e = jnp.asarray(np.kron(np.eye(H), np.ones((DK, DK))), BF16)
    spread = np.zeros((LANE, 2 * hw), np.float32)
    spread[:H, :hw] = np.kron(np.eye(H), np.ones((1, DK)))
    spread[H:2 * H, hw:] = np.kron(np.eye(H), np.ones((1, DK)))
    ex = jnp.asarray(spread, BF16)
    full = lambda shp: _layer_spec(shp, layer)
    const = lambda shp: pl.BlockSpec(shp, lambda b, i: (0, 0))
    blk = lambda unit, w: pl.BlockSpec((tm, w), lambda b, i: (b * nt + i, unit * LANE // w))
    return pl.pallas_call(
        functools.partial(_deltanet_kernel, tm=tm),
        grid=(batch, nt),
        in_specs=[blk(U_QKV, 3 * hw), blk(U_CZ, hw), blk(U_BA, LANE),
                  full((CONV, 3 * hw)), full((1, LANE)), full((1, LANE)), full((1, hw)),
                  const((hw, hw)), const((LANE, 2 * hw))],
        out_specs=pl.BlockSpec((tm, hw), lambda b, i: (b * nt + i, 0)),
        out_shape=jax.ShapeDtypeStruct((n, hw), BF16),
        scratch_shapes=[pltpu.VMEM((8, 3 * hw), F32),
                        pltpu.VMEM((hw // GW, GW, GW), F32)],
        compiler_params=_cparams(("parallel", "arbitrary")),
        name="deltanet",
    )(proj, proj, proj, conv_w, alog, dtb, og, e, ex)


def _mix_ffn_kernel(x_ref, ya_ref, yb_ref, yc_ref, g0_ref, g1_ref, g2_ref, wb_ref, wo_ref, gn_ref,
                    w1_ref, w2_ref, o_ref, a_ref, *, tf):
    merged = None
    for i, (y_ref, g_ref) in enumerate(((ya_ref, g0_ref), (yb_ref, g1_ref), (yc_ref, g2_ref))):
        y = jnp.dot(y_ref[...], wb_ref[i], preferred_element_type=F32)
        term = jax.nn.sigmoid(g_ref[...].astype(F32)) * y
        merged = term if merged is None else merged + term
    x1 = x_ref[...] + jnp.dot(merged.astype(BF16), wo_ref[...], preferred_element_type=F32)
    h = _rms(x1, gn_ref[...]).astype(BF16)
    for f in range(DFF // tf):
        cols = slice(f * tf, (f + 1) * tf)
        a = jnp.dot(h, w1_ref[:, cols], preferred_element_type=F32)
        a_ref[:, cols] = jnp.square(jnp.maximum(a, 0.0)).astype(BF16)
    o_ref[...] = x1 + jnp.dot(a_ref[...], w2_ref[...], preferred_element_type=F32)


def _mix_ffn(x, ya, yb, yc, proj, wb, wo, gn, w1, w2, layer, tm=512, tf=1024):
    n = x.shape[0]
    tok = lambda w: pl.BlockSpec((tm, w), lambda r: (r, 0))
    gate = lambda i: pl.BlockSpec((tm, D), lambda r, i=i: (r, U_GATE // 8 + i))
    res = lambda shp: _layer_spec(shp, layer, single_buffer=True)
    return pl.pallas_call(
        functools.partial(_mix_ffn_kernel, tf=tf),
        grid=(n // tm,),
        in_specs=[tok(D), tok(BW), tok(BW), tok(BW), gate(0), gate(1), gate(2),
                  res((NB, BW, D)), res((D, D)), res((1, D)), res((D, DFF)), res((DFF, D))],
        out_specs=tok(D),
        out_shape=jax.ShapeDtypeStruct((n, D), F32),
        scratch_shapes=[pltpu.VMEM((tm, DFF), BF16)],
        compiler_params=_cparams(("parallel",)),
        name="mix_ffn",
    )(x, ya, yb, yc, proj, proj, proj, wb, wo, gn, w1, w2)


def _pad_heads(w, per_head, first, used):
    depth, kdim, _ = w.shape
    w = w.reshape(depth, kdim, H, per_head)[..., first:first + used]
    w = jnp.pad(w, ((0, 0), (0, 0), (0, 0), (0, LANE - used)))
    return w.reshape(depth, kdim, H * LANE).astype(BF16)


def kernel(x, positions, norm1_g, w_in, sgu_norm_g, w_spatial, b_spatial, q_lat_norm_g, w_q_up,
           kv_lat_norm_g, w_kv_up, q_norm_g, k_norm_g, conv_w, a_log, dt_bias, o_norm_g,
           w_branch, w_out, norm2_g, w_ff1, w_ff2):
    batch, seq, d = x.shape
    depth = w_in.shape[0]
    n = batch * seq
    xt = x.reshape(n, d)
    ctab, stab = _rope_tables(positions.reshape(n, 1))
    rows = lambda a: a.reshape(depth, 1, -1).astype(F32)
    half = ROPE // 2
    w_perm = _permute_w_in(w_in)
    wcat = (w_spatial.reshape(depth, A_G // 2, 2, A_T, A_T).transpose(0, 1, 3, 2, 4)
            .reshape(depth, A_G // 2, A_T, 2 * A_T).astype(F32))
    sgu_bias = jnp.repeat(jnp.swapaxes(b_spatial, 1, 2), A_W // A_G, axis=2).astype(F32)
    wq = _pad_heads(w_q_up, QK, 0, QK)
    wk = _pad_heads(w_kv_up, NOPE + VD, 0, NOPE)
    wv = _pad_heads(w_kv_up, NOPE + VD, NOPE, VD)
    pad_g = lambda g: jnp.pad(
        jnp.stack([g, jnp.concatenate([g[:, :NOPE], g[:, NOPE + half:], g[:, NOPE:NOPE + half]], axis=1)],
                  axis=1).astype(F32), ((0, 0), (0, 0), (0, LANE - QK)))
    qg, kg = pad_g(q_norm_g), pad_g(k_norm_g)
    at_ca = lambda a: jnp.pad(a.astype(F32), ((0, 0), (H, LANE - 2 * H))).reshape(depth, 1, LANE)
    alog, dtb = at_ca(a_log), at_ca(dt_bias)
    og = jnp.tile(o_norm_g.astype(F32), (1, H)).reshape(depth, 1, H * DV)
    conv_f = conv_w.astype(F32)
    wb, wo, w1, w2 = (w.astype(BF16) for w in (w_branch, w_out, w_ff1, w_ff2))
    g1, g2, gs, gq, gkv = (rows(g) for g in (norm1_g, norm2_g, sgu_norm_g, q_lat_norm_g, kv_lat_norm_g))

    for l in range(depth):
        proj, q, k, v, y_a = _inproj_mla(xt, g1, w_perm, ctab, stab, gq, gkv, wq, wk, wv, qg, kg,
                                         gs, wcat, sgu_bias, l)
        y_b = _attention(q, k, v, batch, seq)
        y_c = _deltanet(proj, conv_f, alog, dtb, og, l, batch, seq)
        xt = _mix_ffn(xt, y_a, y_b, y_c, proj, wb, wo, g2, w1, w2, l)
    return xt.reshape(batch, seq, d)
```

```python
import functools
import math

import numpy as np
import jax
import jax.numpy as jnp
from jax import lax
from jax.experimental import pallas as pl
from jax.experimental.pallas import tpu as pltpu

F32 = jnp.float32
BF16 = jnp.bfloat16

LANE = 128
VMEM_LIMIT = 56 * 1024 * 1024

D = 1024
A_W = 512
A_G = 8
A_T = 128
H = 8
NOPE = 64
ROPE = 32
VD = 64
QK = NOPE + ROPE
Q_LORA = 384
KV_LORA = 256
ROPE_BASE = 10000.0
DK = 64
DV = 64
CONV = 4
CC = 64
SB = 16
NB = 3
BW = 512
DFF = 4 * D
EPS = 1e-6

U_MLA, U_BA, U_CZ, U_QKV, U_AU, U_AV, U_GATE = 0, 6, 8, 12, 24, 28, 32
DP = 56 * LANE

O_AU, O_AV, O_QLAT, O_KVLAT, O_KROPE = 0, 512, 1024, 1408, 1664
O_CQKV, O_CZ, O_CB, O_CA, O_GATES = 1696, 3232, 3744, 3752, 3760


def _permute_w_in(w):
    w = w.astype(BF16)
    sl = lambda off, n: w[..., off:off + n]
    zeros = lambda n: jnp.zeros(w.shape[:-1] + (n,), w.dtype)
    pieces = [
        sl(O_QLAT, Q_LORA), sl(O_KVLAT, KV_LORA),
        zeros(NOPE), sl(O_KROPE, ROPE), zeros(LANE - QK),
        sl(O_CB, H), sl(O_CA, H), zeros((U_CZ - U_BA) * LANE - 2 * H),
        sl(O_CZ, 512), sl(O_CQKV, 1536), sl(O_AU, A_W), sl(O_AV, A_W),
        sl(O_GATES, NB * D),
    ]
    out = jnp.concatenate(pieces, axis=-1).astype(BF16)
    assert out.shape[-1] == DP
    return out


def _cparams(sem):
    return pltpu.CompilerParams(dimension_semantics=sem, vmem_limit_bytes=VMEM_LIMIT)


def _layer_spec(shape, layer, single_buffer=False):
    index_map = lambda *_: (layer,) + (0,) * len(shape)
    if single_buffer:
        return pl.BlockSpec((None,) + shape, index_map, pipeline_mode=pl.Buffered(1))
    return pl.BlockSpec((None,) + shape, index_map)


def _rms(x, g):
    ms = jnp.mean(x * x, axis=-1, keepdims=True)
    return x * lax.rsqrt(ms + EPS) * g


def _inproj_mla_kernel(x_ref, g_ref, w_ref, c_ref, s_ref, gq_ref, gkv_ref, wq_ref, wk_ref, wv_ref,
                       qg_ref, kg_ref, gs_ref, ws_ref, bs_ref, o_ref, q_ref, k_ref, v_ref, ya_ref, *, tn):
    h = _rms(x_ref[...], g_ref[...]).astype(BF16)
    assert (U_AU * LANE) % tn == 0 and U_AV * LANE == U_AU * LANE + A_W and tn == 2 * A_W
    sgu_chunk = U_AU * LANE // tn
    pending = [0, sgu_chunk] + [j for j in range(w_ref.shape[1] // tn) if j not in (0, sgu_chunk)]

    def project_chunk():
        if not pending:
            return None
        j = pending.pop(0)
        cols = slice(j * tn, (j + 1) * tn)
        out = jnp.dot(h, w_ref[:, cols], preferred_element_type=F32).astype(o_ref.dtype)
        o_ref[:, cols] = out
        return out

    first = project_chunk()
    uv = project_chunk()
    p = first[:, :Q_LORA + KV_LORA + LANE].astype(F32)
    ql = p[:, :Q_LORA]
    kvl = p[:, Q_LORA:Q_LORA + KV_LORA]
    kr = p[:, Q_LORA + KV_LORA:]
    qn = _rms(ql, gq_ref[...]).astype(BF16)
    kvn = _rms(kvl, gkv_ref[...]).astype(BF16)
    q = jnp.dot(qn, wq_ref[...], preferred_element_type=F32)
    kn = jnp.dot(kvn, wk_ref[...], preferred_element_type=F32)
    v = jnp.dot(kvn, wv_ref[...], preferred_element_type=F32)
    vlane = lax.broadcasted_iota(jnp.int32, v.shape, 1) % LANE
    v_ref[...] = jnp.where(vlane == VD, 1.0, v).astype(v_ref.dtype)
    u, v_lo, v_hi = _sgu_prepare(uv[:, :A_W], uv[:, A_W:], gs_ref)
    project_chunk()
    _sgu_mix(u, v_lo, v_hi, ws_ref, bs_ref, ya_ref)
    cs = c_ref[...]
    sn = s_ref[...]
    lane = lax.broadcasted_iota(jnp.int32, cs.shape, 1)
    upper = lane >= NOPE + ROPE // 2
    scale = QK ** -0.5 * math.log2(math.e)

    q_tabs = (cs * qg_ref[0:1, :], sn * qg_ref[1:2, :])
    k_tabs = (cs * kg_ref[0:1, :], sn * kg_ref[1:2, :])

    tiles = []
    for hd in range(H):
        cols = slice(hd * LANE, (hd + 1) * LANE)
        tiles.append((q_ref, cols, q[:, cols], q_tabs, scale))
        tiles.append((k_ref, cols, kn[:, cols] + kr, k_tabs, 1.0))
    sums = [jnp.sum(x * x, axis=-1, keepdims=True) for _, _, x, _, _ in tiles]
    project_chunk()
    rs = [lax.rsqrt(ss * (1.0 / QK) + EPS) * sc for ss, (_, _, _, _, sc) in zip(sums, tiles)]
    rot = []
    for i, (_, _, x, tabs, _) in enumerate(tiles):
        if i % 4 == 0 and i:
            project_chunk()
        partner = jnp.where(upper, pltpu.roll(x, ROPE // 2, 1), pltpu.roll(x, LANE - ROPE // 2, 1))
        rot.append(x * tabs[0] + partner * tabs[1])
    while pending:
        project_chunk()
    for (ref, cols, _, _, _), y, r in zip(tiles, rot, rs):
        ref[:, cols] = (y * r).astype(ref.dtype)


def _inproj_mla(x, g, w, ctab, stab, gq, gkv, wq, wk, wv, qg, kg, gs, wcat, sgu_bias, layer, tm=512, tn=1024):
    n = x.shape[0]
    dp = w.shape[-1]
    full = lambda shp: _layer_spec(shp, layer)
    tok = lambda width: pl.BlockSpec((tm, width), lambda i: (i, 0))
    return pl.pallas_call(
        functools.partial(_inproj_mla_kernel, tn=tn),
        grid=(n // tm,),
        in_specs=[tok(D), full((1, D)), _layer_spec((D, dp), layer, single_buffer=True),
                  tok(LANE), tok(LANE),
                  full((1, Q_LORA)), full((1, KV_LORA)),
                  full((Q_LORA, H * LANE)), full((KV_LORA, H * LANE)), full((KV_LORA, H * LANE)),
                  full((2, LANE)), full((2, LANE)),
                  full((1, A_W)), full((A_G // 2, A_T, 2 * A_T)), full((A_T, A_W))],
        out_specs=[tok(dp), tok(H * LANE), tok(H * LANE), tok(H * LANE), tok(A_W)],
        out_shape=[jax.ShapeDtypeStruct((n, dp), BF16)] + [jax.ShapeDtypeStruct((n, H * LANE), BF16)] * 3
        + [jax.ShapeDtypeStruct((n, A_W), BF16)],
        compiler_params=_cparams(("parallel",)),
        name="inproj_mla",
    )(x, g, w, ctab, stab, gq, gkv, wq, wk, wv, qg, kg, gs, wcat, sgu_bias)


def _rope_kernel(pos_ref, invf_ref, c_ref, s_ref):
    pos = pos_ref[...].astype(F32)
    ang = pos * invf_ref[...]
    lane = lax.broadcasted_iota(jnp.int32, ang.shape, 1)
    c = jnp.cos(ang)
    s = jnp.sin(ang)
    half = ROPE // 2
    c_ref[...] = jnp.where((lane >= NOPE) & (lane < QK), c, 1.0)
    s_ref[...] = jnp.where(lane < NOPE, 0.0,
                           jnp.where(lane < NOPE + half, -s, jnp.where(lane < QK, s, 0.0)))


def _rope_tables(pos, tm=1024):
    n = pos.shape[0]
    half = ROPE // 2
    lane = np.arange(LANE)
    invf = np.where((lane >= NOPE) & (lane < QK),
                    1.0 / (ROPE_BASE ** (((lane - NOPE) % half).astype(np.float32) / half)), 0.0)
    invf = jnp.asarray(invf, F32).reshape(1, LANE)
    return pl.pallas_call(
        _rope_kernel,
        grid=(n // tm,),
        in_specs=[pl.BlockSpec((tm, 1), lambda i: (i, 0)),
                  pl.BlockSpec((1, LANE), lambda i: (0, 0))],
        out_specs=[pl.BlockSpec((tm, LANE), lambda i: (i, 0)),
                   pl.BlockSpec((tm, LANE), lambda i: (i, 0))],
        out_shape=[jax.ShapeDtypeStruct((n, LANE), F32)] * 2,
        compiler_params=_cparams(("parallel",)),
        name="rope_tables",
    )(pos, invf)


def _gelu(x):
    return 0.5 * x * (1.0 + lax.erf(x * (1.0 / math.sqrt(2.0))))


def _sgu_prepare(a_u, a_v, g_ref):
    u = _gelu(a_u.astype(F32))
    v = _gelu(a_v.astype(F32))
    ms = jnp.mean(v * v, axis=-1, keepdims=True)
    v = v * lax.rsqrt(ms + EPS) * g_ref[...]
    lo = (lax.broadcasted_iota(jnp.int32, v.shape, 1) % LANE) < (A_W // A_G)
    return u, jnp.where(lo, v, 0.0).astype(BF16), jnp.where(lo, 0.0, v).astype(BF16)


def _sgu_mix(u, v_lo, v_hi, w_ref, b_ref, o_ref):
    tm = u.shape[0]
    r = lax.broadcasted_iota(jnp.int32, (A_T, 2 * A_T), 0)
    c = lax.broadcasted_iota(jnp.int32, (A_T, 2 * A_T), 1)
    causal = (c % A_T) <= r
    for p in range(A_G // 2):
        cols = slice(p * LANE, (p + 1) * LANE)
        w = jnp.where(causal, w_ref[p], 0.0).astype(BF16)
        for ch in range(tm // A_T):
            rows = slice(ch * A_T, (ch + 1) * A_T)
            rhs = jnp.concatenate([v_lo[rows, cols], v_hi[rows, cols]], axis=0)
            sv = jnp.dot(w, rhs, preferred_element_type=F32) + b_ref[:, cols]
            o_ref[rows, cols] = (u[rows, cols] * sv).astype(o_ref.dtype)


def _attn_kernel(q_ref, k_ref, v_ref, o_ref, *, t):
    qi = pl.program_id(2)
    heads = (slice(0, LANE), slice(LANE, 2 * LANE))
    qs = [q_ref[:, c] for c in heads]

    def step(start, w, carry, masked):
        ss = [lax.dot_general(q, k_ref[pl.ds(start, w), c], (((1,), (1,)), ((), ())),
                              preferred_element_type=F32) for c, q in zip(heads, qs)]
        if masked:
            row = lax.broadcasted_iota(jnp.int32, (t, w), 0)
            col = lax.broadcasted_iota(jnp.int32, (t, w), 1)
            ok = col <= row + (w - t)
            ss = [jnp.where(ok, s, -jnp.inf) for s in ss]
        ms = [jnp.maximum(m, jnp.max(s, axis=-1, keepdims=True)) for s, (m, _) in zip(ss, carry)]
        ps = [jnp.exp2(s - m_new).astype(BF16) for s, m_new in zip(ss, ms)]
        pv = [jnp.dot(p, v_ref[pl.ds(start, w), c], preferred_element_type=F32) for p, c in zip(ps, heads)]
        return tuple((m_new, jnp.exp2(m - m_new) * acc + x) for m_new, (m, acc), x in zip(ms, carry, pv))

    init = tuple((jnp.full((t, 1), -jnp.inf, F32), jnp.zeros((t, LANE), F32)) for _ in heads)
    carry = lax.fori_loop(
        0, qi // 2, lambda j, c: step(pl.multiple_of(j * (2 * t), 2 * t), 2 * t, c, False), init)
    (_, acc0), (_, acc1) = lax.cond(
        qi % 2 == 1,
        lambda c: step(pl.multiple_of((qi - 1) * t, 2 * t), 2 * t, c, True),
        lambda c: step(pl.multiple_of(qi * t, t), t, c, True),
        carry)
    out0 = acc0 / acc0[:, VD:VD + 1]
    out1 = acc1 / acc1[:, VD:VD + 1]
    lane = lax.broadcasted_iota(jnp.int32, (t, LANE), 1)
    o_ref[...] = jnp.where(lane < VD, out0, pltpu.roll(out1, VD, 1)).astype(o_ref.dtype)


def _attention(q, k, v, batch, seq, t=1024):
    n = q.shape[0]
    nq = seq // t
    k3 = k.reshape(batch, seq, H * LANE)
    v3 = v.reshape(batch, seq, H * LANE)
    return pl.pallas_call(
        functools.partial(_attn_kernel, t=t),
        grid=(batch, H // 2, nq),
        in_specs=[pl.BlockSpec((t, 2 * LANE), lambda b, hp, i: (b * nq + i, hp)),
                  pl.BlockSpec((None, seq, 2 * LANE), lambda b, hp, i: (b, 0, hp)),
                  pl.BlockSpec((None, seq, 2 * LANE), lambda b, hp, i: (b, 0, hp))],
        out_specs=pl.BlockSpec((t, 2 * VD), lambda b, hp, i: (b * nq + i, hp)),
        out_shape=jax.ShapeDtypeStruct((n, H * VD), BF16),
        compiler_params=_cparams(("parallel", "parallel", "arbitrary")),
        name="attention",
    )(q, k3, v3)


GW = 4 * DK


def _blockdiag(x, mask):
    return jnp.where(mask, jnp.concatenate([x] * (GW // CC), axis=0), jnp.zeros((), x.dtype))


def _split3(x):
    hi = x.astype(BF16)
    r = x - hi.astype(F32)
    mid = r.astype(BF16)
    lo = (r - mid.astype(F32)).astype(BF16)
    return hi, mid, lo


def _dot_exact_rhs(sel, x):
    n = x.shape[1]
    r = jnp.dot(sel, jnp.concatenate(_split3(x), axis=1), preferred_element_type=F32)
    return r[:, :n] + r[:, n:2 * n] + r[:, 2 * n:]


def _bd_dot(a, b, mask):
    return jnp.dot(a.astype(BF16), _blockdiag(b.astype(BF16), mask), preferred_element_type=F32)


def _segsum(x, e):
    return jnp.dot(x.astype(BF16), e, preferred_element_type=F32)


def _deltanet_kernel(qkv_ref, z_ref, ba_ref, cw_ref, alog_ref, dtb_ref, og_ref, e_ref, ex_ref,
                     o_ref, tail_ref, state_ref, *, tm):
    t_idx = pl.program_id(1)
    nch = tm // CC
    ng = (H * DK) // GW
    width = 3 * H * DK

    @pl.when(t_idx == 0)
    def _():
        tail_ref[...] = jnp.zeros(tail_ref.shape, F32)
        state_ref[...] = jnp.zeros(state_ref.shape, F32)

    xb = qkv_ref[...]
    x = xb.astype(F32)
    sr = lax.broadcasted_iota(jnp.int32, ((CONV - 1) * tm, tm), 0)
    sc = lax.broadcasted_iota(jnp.int32, ((CONV - 1) * tm, tm), 1)
    sel = jnp.where(sc == (sr % tm) - (CONV - 1 - sr // tm), 1.0, 0.0).astype(BF16)
    shifted = jnp.dot(sel, xb, preferred_element_type=F32)
    xc = cw_ref[CONV - 1:CONV, :] * x
    for j in range(CONV - 1):
        xc = xc + cw_ref[j:j + 1, :] * shifted[j * tm:(j + 1) * tm]
    head = jnp.concatenate([tail_ref[...], x[0:8]], axis=0)
    fix = cw_ref[CONV - 1:CONV, :] * x[0:8]
    for j in range(CONV - 1):
        fix = fix + cw_ref[j:j + 1, :] * head[8 - (CONV - 1) + j: 16 - (CONV - 1) + j]
    xc = jnp.concatenate([fix, xc[8:]], axis=0)
    tail_ref[...] = x[tm - 8:tm]
    xc = xc * jax.nn.sigmoid(xc)
    e = e_ref[...]
    hw = H * DK
    q = xc[:, :hw]
    k = xc[:, hw:2 * hw]
    v = xc[:, 2 * hw:]
    q = q * lax.rsqrt(_segsum(q * q, e) + EPS) * (DK ** -0.5)
    k = k * lax.rsqrt(_segsum(k * k, e) + EPS)
    ba = ba_ref[...].astype(F32)
    sp_in = ba + dtb_ref[...]
    softplus = jnp.maximum(sp_in, 0.0) + jnp.log1p(jnp.exp(-jnp.abs(sp_in)))
    is_b = lax.broadcasted_iota(jnp.int32, ba.shape, 1) < H
    gates = jnp.where(is_b, jax.nn.sigmoid(ba), -jnp.exp(alog_ref[...]) * softplus)
    pieces = jnp.dot(jnp.concatenate(_split3(gates), axis=0), ex_ref[...], preferred_element_type=F32)
    spread = pieces[:tm] + pieces[tm:2 * tm] + pieces[2 * tm:]
    beta = spread[:, :hw]
    g = spread[:, hw:]

    ri = lax.broadcasted_iota(jnp.int32, (tm, tm), 0)
    ci = lax.broadcasted_iota(jnp.int32, (tm, tm), 1)
    cum = jnp.where((ri // CC == ci // CC) & (ci <= ri), 1.0, 0.0).astype(BF16)
    gc = _dot_exact_rhs(cum, g)
    egc = jnp.exp(gc)
    kb = k * beta
    vb = v * beta
    kbg = kb * egc
    qg = q * egc

    r64 = lax.broadcasted_iota(jnp.int32, (CC, GW), 0)
    l64 = lax.broadcasted_iota(jnp.int32, (CC, GW), 1) % CC
    tril = l64 <= r64
    strict = l64 < r64
    eye_t = jnp.where(l64 == r64, 1.0, 0.0).astype(F32)
    rb = lax.broadcasted_iota(jnp.int32, (GW, GW), 0) // CC
    lb = lax.broadcasted_iota(jnp.int32, (GW, GW), 1) // CC
    bd = rb == lb
    ones = jnp.ones((CC, CC), BF16)

    tiles = [(slice(ch * CC, (ch + 1) * CC), slice(gi * GW, (gi + 1) * GW), ch, gi)
             for ch in range(nch) for gi in range(ng)]
    low_l, intra_l = [], []
    for rows, cols, _, _ in tiles:
        gc_c = gc[rows, cols]
        bdk = _blockdiag(k[rows, cols].astype(BF16), bd)
        lhs = jnp.concatenate([kb[rows, cols], q[rows, cols]], axis=0).astype(BF16)
        a2 = lax.dot_general(lhs, bdk, (((1,), (1,)), ((), ())), preferred_element_type=F32)
        gct = _dot_exact_rhs(ones, jnp.where(l64 == r64, gc_c, 0.0))
        decay = jnp.exp(jnp.where(tril, gc_c - gct, -jnp.inf))
        low_l.append(jnp.where(strict, a2[:CC] * decay, 0.0))
        intra_l.append(jnp.where(tril, a2[CC:] * decay, 0.0).astype(BF16))
    n_tiles = len(tiles)
    diag_l = [jnp.where(r64 // SB == l64 // SB, low, 0.0) for low in low_l]
    p_l = [eye_t - d for d in diag_l]
    m_l = [_bd_dot(d, d, bd) for d in diag_l]
    n_sq = int(math.log2(SB)) - 1
    for lvl in range(n_sq):
        for i in range(n_tiles):
            if lvl < n_sq - 1:
                r = _bd_dot(jnp.concatenate([p_l[i], m_l[i]], axis=0), m_l[i], bd)
                p_l[i] = p_l[i] + r[:CC]
                m_l[i] = r[CC:]
            else:
                p_l[i] = p_l[i] + _bd_dot(p_l[i], m_l[i], bd)
    size = SB
    while size < CC:
        off = (r64 // (2 * size) == l64 // (2 * size)) & (r64 // size != l64 // size)
        x_l = [_bd_dot(p, jnp.where(off, low, 0.0), bd) for p, low in zip(p_l, low_l)]
        p_l = [p - _bd_dot(x, p, bd) for p, x in zip(p_l, x_l)]
        size *= 2
    uw_l = []
    for (rows, cols, _, _), p_acc in zip(tiles, p_l):
        rhs = jnp.concatenate([_blockdiag(vb[rows, cols].astype(BF16), bd),
                               _blockdiag(kbg[rows, cols].astype(BF16), bd)], axis=1)
        uw_l.append(jnp.dot(p_acc.astype(BF16), rhs, preferred_element_type=F32))

    states = [state_ref[gi] for gi in range(ng)]
    o_l = []
    for ch in range(nch):
        group = list(zip(tiles, uw_l, intra_l))[ch * ng:(ch + 1) * ng]
        ws_l = [jnp.dot(jnp.concatenate([uw[:, GW:], qg[rows, cols]], axis=0).astype(BF16),
                        states[gi].astype(BF16), preferred_element_type=F32)
                for (rows, cols, _, gi), uw, _ in group]
        for ((rows, cols, _, gi), uw, intra), ws in zip(group, ws_l):
            st = states[gi]
            v_new = (uw[:, :GW] - ws[:CC]).astype(BF16)
            o = ws[CC:] + jnp.dot(intra, _blockdiag(v_new, bd), preferred_element_type=F32)
            g_last = gc[ch * CC + CC - 1: ch * CC + CC, cols]
            kd = (k[rows, cols] * jnp.exp(g_last - gc[rows, cols])).astype(BF16)
            upd = lax.dot_general(kd, v_new, (((0,), (0,)), ((), ())), preferred_element_type=F32)
            states[gi] = st * jnp.exp(g_last) + jnp.where(bd, upd, 0.0)
            o_l.append(o)
    for gi in range(ng):
        state_ref[gi] = states[gi]

    o = jnp.concatenate([jnp.concatenate(o_l[ch * ng:(ch + 1) * ng], axis=1) for ch in range(nch)], axis=0)
    o = o * lax.rsqrt(_segsum(o * o, e) * (1.0 / DV) + EPS) * og_ref[...]
    z = z_ref[...].astype(F32)
    o_ref[...] = (o * (z * jax.nn.sigmoid(z))).astype(o_ref.dtype)


def _deltanet(proj, conv_w, alog, dtb, og, layer, batch, seq, tm=256):
    n = proj.shape[0]
    nt = seq // tm
    hw = H * DK
    e = jnp.asarray(np.kron(np.eye(H), np.ones((DK, DK))), BF16)
    spread = np.zeros((LANE, 2 * hw), np.float32)
    spread[:H, :hw] = np.kron(np.eye(H), np.ones((1, DK)))
    spread[H:2 * H, hw:] = np.kron(np.eye(H), np.ones((1, DK)))
    ex = jnp.asarray(spread, BF16)
    full = lambda shp: _layer_spec(shp, layer)
    const = lambda shp: pl.BlockSpec(shp, lambda b, i: (0, 0))
    blk = lambda unit, w: pl.BlockSpec((tm, w), lambda b, i: (b * nt + i, unit * LANE // w))
    return pl.pallas_call(
        functools.partial(_deltanet_kernel, tm=tm),
        grid=(batch, nt),
        in_specs=[blk(U_QKV, 3 * hw), blk(U_CZ, hw), blk(U_BA, LANE),
                  full((CONV, 3 * hw)), full((1, LANE)), full((1, LANE)), full((1, hw)),
                  const((hw, hw)), const((LANE, 2 * hw))],
        out_specs=pl.BlockSpec((tm, hw), lambda b, i: (b * nt + i, 0)),
        out_shape=jax.ShapeDtypeStruct((n, hw), BF16),
        scratch_shapes=[pltpu.VMEM((8, 3 * hw), F32),
                        pltpu.VMEM((hw // GW, GW, GW), F32)],
        compiler_params=_cparams(("parallel", "arbitrary")),
        name="deltanet",
    )(proj, proj, proj, conv_w, alog, dtb, og, e, ex)


def _mix_ffn_kernel(x_ref, ya_ref, yb_ref, yc_ref, g0_ref, g1_ref, g2_ref, wb_ref, wo_ref, gn_ref,
                    w1_ref, w2_ref, o_ref, a_ref, *, tf):
    merged = None
    for i, (y_ref, g_ref) in enumerate(((ya_ref, g0_ref), (yb_ref, g1_ref), (yc_ref, g2_ref))):
        y = jnp.dot(y_ref[...], wb_ref[i], preferred_element_type=F32)
        term = jax.nn.sigmoid(g_ref[...].astype(F32)) * y
        merged = term if merged is None else merged + term
    x1 = x_ref[...] + jnp.dot(merged.astype(BF16), wo_ref[...], preferred_element_type=F32)
    h = _rms(x1, gn_ref[...]).astype(BF16)
    for f in range(DFF // tf):
        cols = slice(f * tf, (f + 1) * tf)
        a = jnp.dot(h, w1_ref[:, cols], preferred_element_type=F32)
        a_ref[:, cols] = jnp.square(jnp.maximum(a, 0.0)).astype(BF16)
    o_ref[...] = x1 + jnp.dot(a_ref[...], w2_ref[...], preferred_element_type=F32)


def _mix_ffn(x, ya, yb, yc, proj, wb, wo, gn, w1, w2, layer, tm=512, tf=1024):
    n = x.shape[0]
    tok = lambda w: pl.BlockSpec((tm, w), lambda r: (r, 0))
    gate = lambda i: pl.BlockSpec((tm, D), lambda r, i=i: (r, U_GATE // 8 + i))
    res = lambda shp: _layer_spec(shp, layer, single_buffer=True)
    return pl.pallas_call(
        functools.partial(_mix_ffn_kernel, tf=tf),
        grid=(n // tm,),
        in_specs=[tok(D), tok(BW), tok(BW), tok(BW), gate(0), gate(1), gate(2),
                  res((NB, BW, D)), res((D, D)), res((1, D)), res((D, DFF)), res((DFF, D))],
        out_specs=tok(D),
        out_shape=jax.ShapeDtypeStruct((n, D), F32),
        scratch_shapes=[pltpu.VMEM((tm, DFF), BF16)],
        compiler_params=_cparams(("parallel",)),
        name="mix_ffn",
    )(x, ya, yb, yc, proj, proj, proj, wb, wo, gn, w1, w2)


def _pad_heads(w, per_head, first, used):
    depth, kdim, _ = w.shape
    w = w.reshape(depth, kdim, H, per_head)[..., first:first + used]
    w = jnp.pad(w, ((0, 0), (0, 0), (0, 0), (0, LANE - used)))
    return w.reshape(depth, kdim, H * LANE).astype(BF16)


def kernel(x, positions, norm1_g, w_in, sgu_norm_g, w_spatial, b_spatial, q_lat_norm_g, w_q_up,
           kv_lat_norm_g, w_kv_up, q_norm_g, k_norm_g, conv_w, a_log, dt_bias, o_norm_g,
           w_branch, w_out, norm2_g, w_ff1, w_ff2):
    batch, seq, d = x.shape
    depth = w_in.shape[0]
    n = batch * seq
    xt = x.reshape(n, d)
    ctab, stab = _rope_tables(positions.reshape(n, 1))
    rows = lambda a: a.reshape(depth, 1, -1).astype(F32)
    half = ROPE // 2
    w_perm = _permute_w_in(w_in)
    wcat = (w_spatial.reshape(depth, A_G // 2, 2, A_T, A_T).transpose(0, 1, 3, 2, 4)
            .reshape(depth, A_G // 2, A_T, 2 * A_T).astype(F32))
    sgu_bias = jnp.repeat(jnp.swapaxes(b_spatial, 1, 2), A_W // A_G, axis=2).astype(F32)
    wq = _pad_heads(w_q_up, QK, 0, QK)
    wk = _pad_heads(w_kv_up, NOPE + VD, 0, NOPE)
    wv = _pad_heads(w_kv_up, NOPE + VD, NOPE, VD)
    pad_g = lambda g: jnp.pad(
        jnp.stack([g, jnp.concatenate([g[:, :NOPE], g[:, NOPE + half:], g[:, NOPE:NOPE + half]], axis=1)],
                  axis=1).astype(F32), ((0, 0), (0, 0), (0, LANE - QK)))
    qg, kg = pad_g(q_norm_g), pad_g(k_norm_g)
    at_ca = lambda a: jnp.pad(a.astype(F32), ((0, 0), (H, LANE - 2 * H))).reshape(depth, 1, LANE)
    alog, dtb = at_ca(a_log), at_ca(dt_bias)
    og = jnp.tile(o_norm_g.astype(F32), (1, H)).reshape(depth, 1, H * DV)
    conv_f = conv_w.astype(F32)
    wb, wo, w1, w2 = (w.astype(BF16) for w in (w_branch, w_out, w_ff1, w_ff2))
    g1, g2, gs, gq, gkv = (rows(g) for g in (norm1_g, norm2_g, sgu_norm_g, q_lat_norm_g, kv_lat_norm_g))

    for l in range(depth):
        proj, q, k, v, y_a = _inproj_mla(xt, g1, w_perm, ctab, stab, gq, gkv, wq, wk, wv, qg, kg,
                                         gs, wcat, sgu_bias, l)
        y_b = _attention(q, k, v, batch, seq)
        y_c = _deltanet(proj, conv_f, alog, dtb, og, l, batch, seq)
        xt = _mix_ffn(xt, y_a, y_b, y_c, proj, wb, wo, g2, w1, w2, l)
    return xt.reshape(batch, seq, d)
```
